```python
import math
import jax, jax.numpy as jnp
from jax import lax
import numpy as np

D_MODEL = 2048
BATCH = 2
SEQ = 16384
DEPTH = 1

HEAD_DIM = 128
N_SB_HEADS = D_MODEL // (2 * HEAD_DIM)
N_DIFF_HEADS = D_MODEL // (2 * HEAD_DIM)
SB_WIDTH = N_SB_HEADS * HEAD_DIM
DIFF_WIDTH = N_DIFF_HEADS * HEAD_DIM
MIX_WIDTH = SB_WIDTH + DIFF_WIDTH
DIFF_QK_DIM = HEAD_DIM // 2
IN_WIDTH = 3 * SB_WIDTH + 3 * DIFF_WIDTH
D_FF = 2816
D_PLE = 256
ROPE_THETA = 500000.0
ROT_DIM = DIFF_QK_DIM // 4
BLOCK_Q = 128
LN_EPS = 1e-5
RMS_EPS = 1e-5
N_NORMS = 4
DEEPNORM_ALPHA = (2.0 * DEPTH) ** 0.25
DEEPNORM_BETA = (8.0 * DEPTH) ** -0.25

kernel_name = "hybrid_stickbreak_diffattn_macaron_deepnorm"


def _layernorm(x, g, b):
    xf = x.astype(jnp.float32)
    mu = jnp.mean(xf, axis=-1, keepdims=True)
    var = jnp.mean(jnp.square(xf - mu), axis=-1, keepdims=True)
    y = (xf - mu) * lax.rsqrt(var + LN_EPS) * g.astype(jnp.float32) + b.astype(jnp.float32)
    return y.astype(x.dtype)


def _swiglu(x, w_gate, w_up, w_down):
    return (jax.nn.silu(x @ w_gate) * (x @ w_up)) @ w_down


def _heads(t, n_heads, d):
    b, s, _ = t.shape
    return t.reshape(b, s, n_heads, d).transpose(0, 2, 1, 3)


def _rope_partial(t, cos, sin):
    half = ROT_DIM // 2
    t1 = t[..., :half]
    t2 = t[..., half:ROT_DIM]
    return jnp.concatenate([t1 * cos - t2 * sin, t2 * cos + t1 * sin, t[..., ROT_DIM:]], axis=-1)


def _stick_breaking_attention(q, k, v):
    b, h, s, dh = q.shape
    nb = s // BLOCK_Q
    scale = dh ** -0.5
    ar = jnp.arange(BLOCK_Q)
    tri = (ar[:, None] > ar[None, :]).astype(jnp.float32)
    outs = []
    for blk in range(nb):
        n_keys = (blk + 1) * BLOCK_Q
        q_blk = q[:, :, blk * BLOCK_Q:n_keys]
        k_pre = k[:, :, :n_keys]
        v_pre = v[:, :, :n_keys]
        q_idx = blk * BLOCK_Q + ar
        key_idx = jnp.arange(n_keys)
        z = jnp.einsum('bhqd,bhkd->bhqk', q_blk, k_pre).astype(jnp.float32) * scale
        causal = key_idx[None, :] < q_idx[:, None]
        log_beta = jax.nn.log_sigmoid(z)
        log_om = jnp.where(causal, jax.nn.log_sigmoid(-z), 0.0)
        lc = log_om.reshape(b, h, BLOCK_Q, blk + 1, BLOCK_Q)
        within = jnp.einsum('bhqcj,js->bhqcs', lc, tri)
        tot = jnp.sum(lc, axis=-1)
        later = lax.cumsum(tot, axis=3, reverse=True) - tot
        acc = (within + later[..., None]).reshape(b, h, BLOCK_Q, n_keys)
        w = jnp.where(causal, jnp.exp(log_beta + acc), 0.0)
        outs.append(jnp.einsum('bhqk,bhkd->bhqd', w.astype(v.dtype), v_pre))
    return jnp.concatenate(outs, axis=2)


def _differential_attention(q, k, v, lam):
    b, h, _, s, dk = q.shape
    nb = s // BLOCK_Q
    scale = dk ** -0.5
    ar = jnp.arange(BLOCK_Q)
    outs = []
    for blk in range(nb):
        n_keys = (blk + 1) * BLOCK_Q
        q_blk = q[:, :, :, blk * BLOCK_Q:n_keys]
        k_pre = k[:, :, :, :n_keys]
        v_pre = v[:, :, :n_keys]
        q_idx = blk * BLOCK_Q + ar
        key_idx = jnp.arange(n_keys)
        sc = jnp.einsum('bhcqd,bhckd->bhcqk', q_blk, k_pre).astype(jnp.float32) * scale
        causal = key_idx[None, :] <= q_idx[:, None]
        sc = jnp.where(causal, sc, -jnp.inf)
        pr = jax.nn.softmax(sc, axis=-1)
        a = pr[:, :, 0] - lam * pr[:, :, 1]
        outs.append(jnp.einsum('bhqk,bhkd->bhqd', a.astype(v.dtype), v_pre))
    return jnp.concatenate(outs, axis=2)


def _parallel_mixer(h, positions, w_in, w_out, lq1, lk1, lq2, lk2, subln_g, lambda_init):
    b, s, _ = h.shape
    proj = h @ w_in
    sb_q, sb_k, sb_v, df_q, df_k, df_v = jnp.split(proj, 6, axis=-1)

    o_sb = _stick_breaking_attention(_heads(sb_q, N_SB_HEADS, HEAD_DIM),
                                     _heads(sb_k, N_SB_HEADS, HEAD_DIM),
                                     _heads(sb_v, N_SB_HEADS, HEAD_DIM))

    def qk_pair(t):
        return t.reshape(b, s, N_DIFF_HEADS, 2, DIFF_QK_DIM).transpose(0, 2, 3, 1, 4)

    inv_freq = ROPE_THETA ** (-jnp.arange(0, ROT_DIM, 2, dtype=jnp.float32) / ROT_DIM)
    ang = positions.astype(jnp.float32)[:, :, None] * inv_freq
    cos = jnp.cos(ang)[:, None, None].astype(h.dtype)
    sin = jnp.sin(ang)[:, None, None].astype(h.dtype)
    dq = _rope_partial(qk_pair(df_q), cos, sin)
    dk = _rope_partial(qk_pair(df_k), cos, sin)
    dv = _heads(df_v, N_DIFF_HEADS, HEAD_DIM)
    lam = (jnp.exp(jnp.sum(lq1.astype(jnp.float32) * lk1.astype(jnp.float32)))
           - jnp.exp(jnp.sum(lq2.astype(jnp.float32) * lk2.astype(jnp.float32)))
           + lambda_init)
    o_df = _differential_attention(dq, dk, dv, lam).astype(jnp.float32)
    o_df = (o_df * lax.rsqrt(jnp.mean(jnp.square(o_df), axis=-1, keepdims=True) + RMS_EPS)
            * subln_g.astype(jnp.float32) * (1.0 - lambda_init)).astype(h.dtype)

    o = jnp.concatenate([o_sb.astype(h.dtype), o_df], axis=1)
    o = o.transpose(0, 2, 1, 3).reshape(b, s, MIX_WIDTH)
    return o @ w_out


def setup_inputs(seed: int = 0) -> dict:
    key = jax.random.key(seed)
    ks = jax.random.split(key, 24)
    f32 = jnp.float32
    nrm = lambda k, shape: jax.random.normal(k, shape, dtype=f32)
    x = nrm(ks[0], (BATCH, SEQ, D_MODEL))
    p = nrm(ks[1], (DEPTH, BATCH, SEQ, D_PLE))
    offsets = jax.random.randint(ks[2], (BATCH, 1), 0, 1024, dtype=jnp.int32)
    positions = (jnp.arange(SEQ, dtype=jnp.int32)[None, :] + offsets).astype(jnp.int32)
    ln_g = 1.0 + 0.02 * nrm(ks[3], (DEPTH, N_NORMS, D_MODEL))
    ln_b = 0.02 * nrm(ks[4], (DEPTH, N_NORMS, D_MODEL))
    sd = D_MODEL ** -0.5
    sf = D_FF ** -0.5
    ffn1_w_gate = nrm(ks[5], (DEPTH, D_MODEL, D_FF)) * sd
    ffn1_w_up = nrm(ks[6], (DEPTH, D_MODEL, D_FF)) * sd
    ffn1_w_down = nrm(ks[7], (DEPTH, D_FF, D_MODEL)) * sf * DEEPNORM_BETA
    col_scale = jnp.concatenate([
        jnp.ones((2 * SB_WIDTH,), f32), jnp.full((SB_WIDTH,), DEEPNORM_BETA, f32),
        jnp.ones((2 * DIFF_WIDTH,), f32), jnp.full((DIFF_WIDTH,), DEEPNORM_BETA, f32)])
    w_in = nrm(ks[8], (DEPTH, D_MODEL, IN_WIDTH)) * sd * col_scale
    w_out = nrm(ks[9], (DEPTH, MIX_WIDTH, D_MODEL)) * (MIX_WIDTH ** -0.5) * DEEPNORM_BETA
    lambda_q1 = 0.1 * nrm(ks[10], (DEPTH, DIFF_QK_DIM))
    lambda_k1 = 0.1 * nrm(ks[11], (DEPTH, DIFF_QK_DIM))
    lambda_q2 = 0.1 * nrm(ks[12], (DEPTH, DIFF_QK_DIM))
    lambda_k2 = 0.1 * nrm(ks[13], (DEPTH, DIFF_QK_DIM))
    diff_subln_g = 1.0 + 0.02 * nrm(ks[14], (DEPTH, HEAD_DIM))
    ffn2_w_gate = nrm(ks[15], (DEPTH, D_MODEL, D_FF)) * sd
    ffn2_w_up = nrm(ks[16], (DEPTH, D_MODEL, D_FF)) * sd
    ffn2_w_down = nrm(ks[17], (DEPTH, D_FF, D_MODEL)) * sf * DEEPNORM_BETA
    ple_w_gate = nrm(ks[18], (DEPTH, D_MODEL, D_MODEL)) * sd
    ple_b_gate = 0.02 * nrm(ks[19], (DEPTH, D_MODEL))
    ple_w_proj = nrm(ks[20], (DEPTH, D_PLE, D_MODEL)) * (D_PLE ** -0.5) * DEEPNORM_BETA
    return {"x": x, "p": p, "positions": positions, "ln_g": ln_g, "ln_b": ln_b,
            "ffn1_w_gate": ffn1_w_gate, "ffn1_w_up": ffn1_w_up, "ffn1_w_down": ffn1_w_down,
            "w_in": w_in, "w_out": w_out,
            "lambda_q1": lambda_q1, "lambda_k1": lambda_k1,
            "lambda_q2": lambda_q2, "lambda_k2": lambda_k2, "diff_subln_g": diff_subln_g,
            "ffn2_w_gate": ffn2_w_gate, "ffn2_w_up": ffn2_w_up, "ffn2_w_down": ffn2_w_down,
            "ple_w_gate": ple_w_gate, "ple_b_gate": ple_b_gate, "ple_w_proj": ple_w_proj}


def reference(x, p, positions, ln_g, ln_b, ffn1_w_gate, ffn1_w_up, ffn1_w_down,
              w_in, w_out, lambda_q1, lambda_k1, lambda_q2, lambda_k2, diff_subln_g,
              ffn2_w_gate, ffn2_w_up, ffn2_w_down, ple_w_gate, ple_b_gate, ple_w_proj):
    h = x
    for i in range(DEPTH):
        lambda_init = 0.8 - 0.6 * math.exp(-0.3 * i)
        h = _layernorm(DEEPNORM_ALPHA * h
                       + 0.5 * _swiglu(h, ffn1_w_gate[i], ffn1_w_up[i], ffn1_w_down[i]),
                       ln_g[i, 0], ln_b[i, 0])
        mix = _parallel_mixer(h, positions, w_in[i], w_out[i],
                              lambda_q1[i], lambda_k1[i], lambda_q2[i], lambda_k2[i],
                              diff_subln_g[i], lambda_init)
        h = _layernorm(DEEPNORM_ALPHA * h + mix, ln_g[i, 1], ln_b[i, 1])
        h = _layernorm(DEEPNORM_ALPHA * h
                       + 0.5 * _swiglu(h, ffn2_w_gate[i], ffn2_w_up[i], ffn2_w_down[i]),
                       ln_g[i, 2], ln_b[i, 2])
        gate = jax.nn.sigmoid(h @ ple_w_gate[i] + ple_b_gate[i])
        h = _layernorm(DEEPNORM_ALPHA * h + gate * (p[i] @ ple_w_proj[i]),
                       ln_g[i, 3], ln_b[i, 3])
    return h
```

```python
import functools
import math

import jax
import jax.numpy as jnp
from jax import lax
from jax.experimental import pallas as pl
from jax.experimental.pallas import tpu as pltpu

HEAD_DIM = 128
DIFF_QK_DIM = HEAD_DIM // 2
ROT_DIM = DIFF_QK_DIM // 4
ROPE_THETA = 500000.0
LN_EPS = 1e-5
RMS_EPS = 1e-5

LANES = 128
VMEM_LIMIT_BYTES = 56 * 1024 * 1024

BF16 = jnp.bfloat16
F32 = jnp.float32

_NT = (((1,), (1,)), ((), ()))


def _dot(a, b):
    return jnp.dot(a, b, preferred_element_type=F32)


def _dot_nt(a, b):
    return lax.dot_general(a, b, _NT, preferred_element_type=F32)


def _layernorm_rows(y, g, b):
    mu = jnp.mean(y, axis=-1, keepdims=True)
    yc = y - mu
    var = jnp.mean(yc * yc, axis=-1, keepdims=True)
    return yc * lax.rsqrt(var + LN_EPS) * g + b


def _params(*semantics):
    return pltpu.CompilerParams(dimension_semantics=semantics,
                                vmem_limit_bytes=VMEM_LIMIT_BYTES)


def _ffn_ln_kernel(x_ref, wg_ref, wu_ref, wd_ref, g_ref, b_ref, o_ref, xb_ref, *, alpha):
    j = pl.program_id(1)

    @pl.when(j == 0)
    def _():
        xb_ref[...] = x_ref[...].astype(BF16)
        o_ref[...] = jnp.zeros_like(o_ref)

    xb = xb_ref[...]
    gate = _dot(xb, wg_ref[...])
    up = _dot(xb, wu_ref[...])
    act = (gate * jax.nn.sigmoid(gate) * up).astype(BF16)
    o_ref[...] += _dot(act, wd_ref[...])

    @pl.when(j == pl.num_programs(1) - 1)
    def _():
        y = alpha * x_ref[...] + 0.5 * o_ref[...]
        o_ref[...] = _layernorm_rows(y, g_ref[...], b_ref[...])


def _ffn_ln(h, wg, wu, wd, g, b, *, alpha, tm, tf):
    n, d = h.shape
    f = wg.shape[1]
    return pl.pallas_call(
        functools.partial(_ffn_ln_kernel, alpha=alpha),
        grid=(n // tm, f // tf),
        in_specs=[
            pl.BlockSpec((tm, d), lambda i, j: (i, 0)),
            pl.BlockSpec((d, tf), lambda i, j: (0, j)),
            pl.BlockSpec((d, tf), lambda i, j: (0, j)),
            pl.BlockSpec((tf, d), lambda i, j: (j, 0)),
            pl.BlockSpec((1, d), lambda i, j: (0, 0)),
            pl.BlockSpec((1, d), lambda i, j: (0, 0)),
        ],
        out_specs=pl.BlockSpec((tm, d), lambda i, j: (i, 0)),
        out_shape=jax.ShapeDtypeStruct((n, d), F32),
        scratch_shapes=[pltpu.VMEM((tm, d), BF16)],
        compiler_params=_params("parallel", "arbitrary"),
        name="ffn_ln",
    )(h, wg, wu, wd, g, b)


def _qkv_kernel(h_ref, pos_ref, freq_ref, w_ref, o_ref, hb_ref, c_ref, sa_ref, sb_ref,
                *, sb_scale, diff_scale):
    j = pl.program_id(1)

    @pl.when(j == 0)
    def _():
        hb_ref[...] = h_ref[...].astype(BF16)
        lane = lax.broadcasted_iota(jnp.int32, (1, LANES), 1) % DIFF_QK_DIM
        ang = pos_ref[...].astype(F32) * freq_ref[...]
        cos = jnp.cos(ang)
        sin = jnp.sin(ang)
        half = ROT_DIM // 2
        c_ref[...] = jnp.where(lane < ROT_DIM, cos, 1.0)
        sa_ref[...] = jnp.where(lane < half, -sin, 0.0)
        sb_ref[...] = jnp.where((lane >= half) & (lane < ROT_DIM), sin, 0.0)

    t = _dot(hb_ref[...], w_ref[...])
    tn = t.shape[1]

    def rope(t):
        reps = tn // LANES
        c = jnp.tile(c_ref[...], (1, reps))
        sa = jnp.tile(sa_ref[...], (1, reps))
        sb = jnp.tile(sb_ref[...], (1, reps))
        half = ROT_DIM // 2
        t_up = pltpu.roll(t, tn - half, axis=1)
        t_dn = pltpu.roll(t, half, axis=1)
        return t * c + t_up * sa + t_dn * sb

    @pl.when(j == 0)
    def _():
        o_ref[...] = (t * sb_scale).astype(o_ref.dtype)

    @pl.when((j == 1) | (j == 2) | (j == 5))
    def _():
        o_ref[...] = t.astype(o_ref.dtype)

    @pl.when(j == 3)
    def _():
        o_ref[...] = (rope(t) * diff_scale).astype(o_ref.dtype)

    @pl.when(j == 4)
    def _():
        o_ref[...] = rope(t).astype(o_ref.dtype)


def _qkv_proj(h, pos, freq, w_in, *, tm):
    n, d = h.shape
    width = w_in.shape[1]
    tn = width // 6
    return pl.pallas_call(
        functools.partial(_qkv_kernel, sb_scale=HEAD_DIM ** -0.5, diff_scale=DIFF_QK_DIM ** -0.5),
        grid=(n // tm, 6),
        in_specs=[
            pl.BlockSpec((tm, d), lambda i, j: (i, 0)),
            pl.BlockSpec((tm, 1), lambda i, j: (i, 0)),
            pl.BlockSpec((1, LANES), lambda i, j: (0, 0)),
            pl.BlockSpec((d, tn), lambda i, j: (0, j)),
        ],
        out_specs=pl.BlockSpec((tm, tn), lambda i, j: (i, j)),
        out_shape=jax.ShapeDtypeStruct((n, width), BF16),
        scratch_shapes=[pltpu.VMEM((tm, d), BF16),
                        pltpu.VMEM((tm, LANES), F32),
                        pltpu.VMEM((tm, LANES), F32),
                        pltpu.VMEM((tm, LANES), F32)],
        compiler_params=_params("parallel", "arbitrary"),
        name="qkv_proj",
    )(h, pos, freq, w_in)


def _sb_attn_kernel(q_ref, k_ref, v_ref, o_ref, *, tq):
    i = pl.program_id(2)
    q = q_ref[0]
    row = lax.broadcasted_iota(jnp.int32, (tq, tq), 0)
    col = lax.broadcasted_iota(jnp.int32, (tq, tq), 1)
    tri = jnp.where(row > col, 1.0, 0.0).astype(BF16)

    def block(c, carry, acc, diagonal):
        start = pl.multiple_of(c * tq, tq)
        k = k_ref[0, pl.ds(start, tq), :]
        v = v_ref[0, pl.ds(start, tq), :]
        z = _dot_nt(q, k)
        softplus = jnp.log(1.0 + jnp.exp(-jnp.abs(z)))
        log_beta = jnp.minimum(z, 0.0) - softplus
        log_om = log_beta - z
        if diagonal:
            causal = col < row
            log_om = jnp.where(causal, log_om, 0.0)
        within = _dot(log_om.astype(BF16), tri)
        w = jnp.exp(log_beta + within + carry)
        if diagonal:
            w = jnp.where(causal, w, 0.0)
        acc = acc + _dot(w.astype(BF16), v)
        carry = carry + jnp.sum(log_om, axis=-1, keepdims=True)
        return carry, acc

    carry = jnp.zeros((tq, 1), F32)
    acc = jnp.zeros((tq, HEAD_DIM), F32)
    carry, acc = block(i, carry, acc, True)

    def body(t, state):
        return block(i - 1 - t, state[0], state[1], False)

    carry, acc = lax.fori_loop(0, i, body, (carry, acc))
    o_ref[0] = acc.astype(o_ref.dtype)


def _sb_attention(qkv, *, n_heads, tq):
    b, s, _ = qkv.shape
    return pl.pallas_call(
        functools.partial(_sb_attn_kernel, tq=tq),
        grid=(b, n_heads, s // tq),
        in_specs=[
            pl.BlockSpec((1, tq, HEAD_DIM), lambda bi, h, i: (bi, i, h)),
            pl.BlockSpec((1, s, HEAD_DIM), lambda bi, h, i: (bi, 0, n_heads + h)),
            pl.BlockSpec((1, s, HEAD_DIM), lambda bi, h, i: (bi, 0, 2 * n_heads + h)),
        ],
        out_specs=pl.BlockSpec((1, tq, HEAD_DIM), lambda bi, h, i: (bi, i, h)),
        out_shape=jax.ShapeDtypeStruct((b, s, n_heads * HEAD_DIM), BF16),
        compiler_params=_params("parallel", "parallel", "arbitrary"),
        name="sb_attn",
    )(qkv, qkv, qkv)


def _diff_attn_kernel(q_ref, k_ref, v_ref, lq1_ref, lk1_ref, lq2_ref, lk2_ref, g_ref, o_ref,
                      *, tq, lambda_init):
    i = pl.program_id(2)
    q = q_ref[0]
    lane = lax.broadcasted_iota(jnp.int32, (tq, HEAD_DIM), 1)
    zero = jnp.zeros_like(q)
    qs = (jnp.where(lane < DIFF_QK_DIM, q, zero), jnp.where(lane >= DIFF_QK_DIM, q, zero))
    row = lax.broadcasted_iota(jnp.int32, (tq, tq), 0)
    col = lax.broadcasted_iota(jnp.int32, (tq, tq), 1)

    def block(c, state, diagonal):
        start = pl.multiple_of(c * tq, tq)
        k = k_ref[0, pl.ds(start, tq), :]
        v = v_ref[0, pl.ds(start, tq), :]
        new = []
        for comp in range(2):
            m, l, acc = state[comp]
            sc = _dot_nt(qs[comp], k)
            if diagonal:
                sc = jnp.where(col <= row, sc, -jnp.inf)
            m_new = jnp.maximum(m, jnp.max(sc, axis=-1, keepdims=True))
            scale = jnp.exp(m - m_new)
            pr = jnp.exp(sc - m_new)
            l = scale * l + jnp.sum(pr, axis=-1, keepdims=True)
            acc = scale * acc + _dot(pr.astype(BF16), v)
            new.append((m_new, l, acc))
        return tuple(new)

    init = tuple((jnp.full((tq, 1), -jnp.inf, F32), jnp.zeros((tq, 1), F32),
                  jnp.zeros((tq, HEAD_DIM), F32)) for _ in range(2))
    state = block(i, init, True)
    state = lax.fori_loop(0, i, lambda t, st: block(i - 1 - t, st, False), state)

    lam = (jnp.exp(jnp.sum(lq1_ref[...] * lk1_ref[...], axis=-1, keepdims=True))
           - jnp.exp(jnp.sum(lq2_ref[...] * lk2_ref[...], axis=-1, keepdims=True))
           + lambda_init)
    (_, l1, acc1), (_, l2, acc2) = state
    o = acc1 / l1 - lam * (acc2 / l2)
    o = o * lax.rsqrt(jnp.mean(o * o, axis=-1, keepdims=True) + RMS_EPS)
    o_ref[0] = (o * g_ref[...] * (1.0 - lambda_init)).astype(o_ref.dtype)


def _diff_attention(qkv, lq1, lk1, lq2, lk2, subln_g, *, n_heads, col0, lambda_init, tq):
    b, s, _ = qkv.shape
    small = lambda width: pl.BlockSpec((1, width), lambda bi, h, i: (0, 0))
    return pl.pallas_call(
        functools.partial(_diff_attn_kernel, tq=tq, lambda_init=lambda_init),
        grid=(b, n_heads, s // tq),
        in_specs=[
            pl.BlockSpec((1, tq, HEAD_DIM), lambda bi, h, i: (bi, i, col0 + h)),
            pl.BlockSpec((1, s, HEAD_DIM), lambda bi, h, i: (bi, 0, col0 + n_heads + h)),
            pl.BlockSpec((1, s, HEAD_DIM), lambda bi, h, i: (bi, 0, col0 + 2 * n_heads + h)),
            small(DIFF_QK_DIM), small(DIFF_QK_DIM), small(DIFF_QK_DIM), small(DIFF_QK_DIM),
            small(HEAD_DIM),
        ],
        out_specs=pl.BlockSpec((1, tq, HEAD_DIM), lambda bi, h, i: (bi, i, h)),
        out_shape=jax.ShapeDtypeStruct((b, s, n_heads * HEAD_DIM), BF16),
        compiler_params=_params("parallel", "parallel", "arbitrary"),
        name="diff_attn",
    )(qkv, qkv, qkv, lq1, lk1, lq2, lk2, subln_g)


def _outproj_ln_kernel(h_ref, osb_ref, odf_ref, wa_ref, wb_ref, g_ref, b_ref, o_ref, *, alpha):
    mix = _dot(osb_ref[...], wa_ref[...]) + _dot(odf_ref[...], wb_ref[...])
    o_ref[...] = _layernorm_rows(alpha * h_ref[...] + mix, g_ref[...], b_ref[...])


def _outproj_ln(h, o_sb, o_df, w_a, w_b, g, b, *, alpha, tm):
    n, d = h.shape
    wa_rows, wb_rows = w_a.shape[0], w_b.shape[0]
    return pl.pallas_call(
        functools.partial(_outproj_ln_kernel, alpha=alpha),
        grid=(n // tm,),
        in_specs=[
            pl.BlockSpec((tm, d), lambda i: (i, 0)),
            pl.BlockSpec((tm, wa_rows), lambda i: (i, 0)),
            pl.BlockSpec((tm, wb_rows), lambda i: (i, 0)),
            pl.BlockSpec((wa_rows, d), lambda i: (0, 0)),
            pl.BlockSpec((wb_rows, d), lambda i: (0, 0)),
            pl.BlockSpec((1, d), lambda i: (0, 0)),
            pl.BlockSpec((1, d), lambda i: (0, 0)),
        ],
        out_specs=pl.BlockSpec((tm, d), lambda i: (i, 0)),
        out_shape=jax.ShapeDtypeStruct((n, d), F32),
        compiler_params=_params("parallel"),
        name="outproj_ln",
    )(h, o_sb, o_df, w_a, w_b, g, b)


def _ple_ln_kernel(h_ref, p_ref, wg_ref, bg_ref, wp_ref, g_ref, b_ref, o_ref, *, alpha):
    h = h_ref[...]
    gate = jax.nn.sigmoid(_dot(h.astype(BF16), wg_ref[...]) + bg_ref[...])
    emb = _dot(p_ref[...].astype(BF16), wp_ref[...])
    o_ref[...] = _layernorm_rows(alpha * h + gate * emb, g_ref[...], b_ref[...])


def _ple_ln(h, p, wg, bg, wp, g, b, *, alpha, tm):
    n, d = h.shape
    dp = p.shape[1]
    return pl.pallas_call(
        functools.partial(_ple_ln_kernel, alpha=alpha),
        grid=(n // tm,),
        in_specs=[
            pl.BlockSpec((tm, d), lambda i: (i, 0)),
            pl.BlockSpec((tm, dp), lambda i: (i, 0)),
            pl.BlockSpec((d, d), lambda i: (0, 0)),
            pl.BlockSpec((1, d), lambda i: (0, 0)),
            pl.BlockSpec((dp, d), lambda i: (0, 0)),
            pl.BlockSpec((1, d), lambda i: (0, 0)),
            pl.BlockSpec((1, d), lambda i: (0, 0)),
        ],
        out_specs=pl.BlockSpec((tm, d), lambda i: (i, 0)),
        out_shape=jax.ShapeDtypeStruct((n, d), F32),
        compiler_params=_params("parallel"),
        name="ple_ln",
    )(h, p, wg, bg, wp, g, b)


def _tile(n, target):
    t = min(n, target)
    assert n % t == 0, (n, t)
    return t


def kernel(x, p, positions, ln_g, ln_b, ffn1_w_gate, ffn1_w_up, ffn1_w_down, w_in, w_out,
           lambda_q1, lambda_k1, lambda_q2, lambda_k2, diff_subln_g,
           ffn2_w_gate, ffn2_w_up, ffn2_w_down, ple_w_gate, ple_b_gate, ple_w_proj):
    batch, seq, d = x.shape
    depth = ln_g.shape[0]
    n = batch * seq
    n_sb = d // (2 * HEAD_DIM)
    n_diff = d // (2 * HEAD_DIM)
    sb_width = n_sb * HEAD_DIM
    alpha = (2.0 * depth) ** 0.25

    tm_ffn = _tile(n, 512)
    tf = _tile(ffn1_w_gate.shape[2], 256)
    tm_proj = _tile(n, 512)
    tq = _tile(seq, 256)

    inv_freq = ROPE_THETA ** (-jnp.arange(0, ROT_DIM, 2, dtype=F32) / ROT_DIM)
    lane = jnp.arange(LANES) % DIFF_QK_DIM
    freq = jnp.where(lane < ROT_DIM, inv_freq[lane % (ROT_DIM // 2)], 0.0).reshape(1, LANES)
    pos = positions.reshape(n, 1)

    row = lambda v: v.reshape(1, -1)
    h = x.reshape(n, d)
    for i in range(depth):
        lambda_init = 0.8 - 0.6 * math.exp(-0.3 * i)
        h = _ffn_ln(h, ffn1_w_gate[i].astype(BF16), ffn1_w_up[i].astype(BF16),
                    ffn1_w_down[i].astype(BF16), row(ln_g[i, 0]), row(ln_b[i, 0]),
                    alpha=alpha, tm=tm_ffn, tf=tf)
        qkv = _qkv_proj(h, pos, freq, w_in[i].astype(BF16), tm=tm_proj)
        qkv = qkv.reshape(batch, seq, -1)
        o_sb = _sb_attention(qkv, n_heads=n_sb, tq=tq)
        o_df = _diff_attention(qkv, row(lambda_q1[i]), row(lambda_k1[i]), row(lambda_q2[i]),
                               row(lambda_k2[i]), row(diff_subln_g[i]), n_heads=n_diff,
                               col0=3 * n_sb, lambda_init=lambda_init, tq=tq)
        w_o = w_out[i].astype(BF16)
        h = _outproj_ln(h, o_sb.reshape(n, -1), o_df.reshape(n, -1), w_o[:sb_width], w_o[sb_width:],
                        row(ln_g[i, 1]), row(ln_b[i, 1]), alpha=alpha, tm=tm_proj)
        h = _ffn_ln(h, ffn2_w_gate[i].astype(BF16), ffn2_w_up[i].astype(BF16),
                    ffn2_w_down[i].astype(BF16), row(ln_g[i, 2]), row(ln_b[i, 2]),
                    alpha=alpha, tm=tm_ffn, tf=tf)
        h = _ple_ln(h, p[i].reshape(n, -1), ple_w_gate[i].astype(BF16), row(ple_b_gate[i]),
                    ple_w_proj[i].astype(BF16), row(ln_g[i, 3]), row(ln_b[i, 3]),
                    alpha=alpha, tm=tm_proj)
    return h.reshape(batch, seq, d)
```

```python
import functools
import math

import jax
import jax.numpy as jnp
from jax import lax
from jax.experimental import pallas as pl
from jax.experimental.pallas import tpu as pltpu

HEAD_DIM = 128
DIFF_QK_DIM = HEAD_DIM // 2
ROT_DIM = DIFF_QK_DIM // 4
ROPE_THETA = 500000.0
LN_EPS = 1e-5
RMS_EPS = 1e-5

LANES = 128
VMEM_LIMIT_BYTES = 56 * 1024 * 1024

BF16 = jnp.bfloat16
F32 = jnp.float32

LOG2E = 1.4426950408889634

_NT = (((1,), (1,)), ((), ()))


def _dot(a, b):
    return jnp.dot(a, b, preferred_element_type=F32)


def _dot_nt(a, b):
    return lax.dot_general(a, b, _NT, preferred_element_type=F32)


def _layernorm_rows(y, g, b):
    mu = jnp.mean(y, axis=-1, keepdims=True)
    yc = y - mu
    var = jnp.mean(yc * yc, axis=-1, keepdims=True)
    return yc * lax.rsqrt(var + LN_EPS) * g + b


def _params(*semantics):
    return pltpu.CompilerParams(dimension_semantics=semantics,
                                vmem_limit_bytes=VMEM_LIMIT_BYTES)


def _ffn_ln_kernel(x_ref, wg_ref, wu_ref, wd_ref, g_ref, b_ref, o_ref, xb_ref, *, alpha):
    j = pl.program_id(1)

    @pl.when(j == 0)
    def _():
        xb_ref[...] = x_ref[...].astype(BF16)
        o_ref[...] = jnp.zeros_like(o_ref)

    xb = xb_ref[...]
    gate = _dot(xb, wg_ref[...])
    up = _dot(xb, wu_ref[...])
    act = (gate * jax.nn.sigmoid(gate) * up).astype(BF16)
    o_ref[...] += _dot(act, wd_ref[...])

    @pl.when(j == pl.num_programs(1) - 1)
    def _():
        y = alpha * x_ref[...] + 0.5 * o_ref[...]
        o_ref[...] = _layernorm_rows(y, g_ref[...], b_ref[...])


def _ffn_ln(h, wg, wu, wd, g, b, *, alpha, tm, tf):
    n, d = h.shape
    f = wg.shape[1]
    return pl.pallas_call(
        functools.partial(_ffn_ln_kernel, alpha=alpha),
        grid=(n // tm, f // tf),
        in_specs=[
            pl.BlockSpec((tm, d), lambda i, j: (i, 0)),
            pl.BlockSpec((d, tf), lambda i, j: (0, j)),
            pl.BlockSpec((d, tf), lambda i, j: (0, j)),
            pl.BlockSpec((tf, d), lambda i, j: (j, 0)),
            pl.BlockSpec((1, d), lambda i, j: (0, 0)),
            pl.BlockSpec((1, d), lambda i, j: (0, 0)),
        ],
        out_specs=pl.BlockSpec((tm, d), lambda i, j: (i, 0)),
        out_shape=jax.ShapeDtypeStruct((n, d), F32),
        scratch_shapes=[pltpu.VMEM((tm, d), BF16)],
        compiler_params=_params("parallel", "arbitrary"),
        name="ffn_ln",
    )(h, wg, wu, wd, g, b)


def _qkv_kernel(h_ref, pos_ref, freq_ref, w_ref, o_ref, hb_ref, c_ref, sa_ref, sb_ref,
                *, sb_scale, diff_scale):
    j = pl.program_id(1)

    @pl.when(j == 0)
    def _():
        hb_ref[...] = h_ref[...].astype(BF16)
        lane = lax.broadcasted_iota(jnp.int32, (1, LANES), 1) % DIFF_QK_DIM
        ang = pos_ref[...].astype(F32) * freq_ref[...]
        cos = jnp.cos(ang)
        sin = jnp.sin(ang)
        half = ROT_DIM // 2
        c_ref[...] = jnp.where(lane < ROT_DIM, cos, 1.0)
        sa_ref[...] = jnp.where(lane < half, -sin, 0.0)
        sb_ref[...] = jnp.where((lane >= half) & (lane < ROT_DIM), sin, 0.0)

    t = _dot(hb_ref[...], w_ref[...])
    tn = t.shape[1]

    def rope(t):
        reps = tn // LANES
        c = jnp.tile(c_ref[...], (1, reps))
        sa = jnp.tile(sa_ref[...], (1, reps))
        sb = jnp.tile(sb_ref[...], (1, reps))
        half = ROT_DIM // 2
        t_up = pltpu.roll(t, tn - half, axis=1)
        t_dn = pltpu.roll(t, half, axis=1)
        return t * c + t_up * sa + t_dn * sb

    @pl.when(j == 0)
    def _():
        o_ref[...] = (t * sb_scale).astype(o_ref.dtype)

    @pl.when((j == 1) | (j == 2) | (j == 5))
    def _():
        o_ref[...] = t.astype(o_ref.dtype)

    @pl.when(j == 3)
    def _():
        o_ref[...] = (rope(t) * diff_scale).astype(o_ref.dtype)

    @pl.when(j == 4)
    def _():
        o_ref[...] = rope(t).astype(o_ref.dtype)


def _qkv_proj(h, pos, freq, w_in, *, tm):
    n, d = h.shape
    width = w_in.shape[1]
    tn = width // 6
    return pl.pallas_call(
        functools.partial(_qkv_kernel, sb_scale=HEAD_DIM ** -0.5 * LOG2E,
                          diff_scale=DIFF_QK_DIM ** -0.5 * LOG2E),
        grid=(n // tm, 6),
        in_specs=[
            pl.BlockSpec((tm, d), lambda i, j: (i, 0)),
            pl.BlockSpec((tm, 1), lambda i, j: (i, 0)),
            pl.BlockSpec((1, LANES), lambda i, j: (0, 0)),
            pl.BlockSpec((d, tn), lambda i, j: (0, j)),
        ],
        out_specs=pl.BlockSpec((tm, tn), lambda i, j: (i, j)),
        out_shape=jax.ShapeDtypeStruct((n, width), BF16),
        scratch_shapes=[pltpu.VMEM((tm, d), BF16),
                        pltpu.VMEM((tm, LANES), F32),
                        pltpu.VMEM((tm, LANES), F32),
                        pltpu.VMEM((tm, LANES), F32)],
        compiler_params=_params("parallel", "arbitrary"),
        name="qkv_proj",
    )(h, pos, freq, w_in)


def _neg_abs(x):
    bits = lax.bitcast_convert_type(x, jnp.uint32) | jnp.uint32(0x80000000)
    return lax.bitcast_convert_type(bits, F32)


def _sb_attn_kernel(q_ref, k_ref, v_ref, o_ref, acc_ref, carry_ref, *, tq, tk):
    i = pl.program_id(2)
    ratio = tq // tk
    row = lax.broadcasted_iota(jnp.int32, (tk, tk), 0)
    col = lax.broadcasted_iota(jnp.int32, (tk, tk), 1)
    tri = jnp.where(row > col, 1.0, 0.0).astype(BF16)

    def block(c, r0, masked):
        start = pl.multiple_of(c * tk, tk)
        k = k_ref[0, pl.ds(start, tk), :]
        v = v_ref[0, pl.ds(start, tk), :]
        z = _dot_nt(q_ref[0, r0:, :], k)
        softplus = jnp.log(1.0 + jnp.exp2(_neg_abs(z))) * LOG2E
        log_beta = jnp.minimum(z, 0.0) - softplus
        log_om = log_beta - z
        if masked:
            rows = tq - r0
            causal = (lax.broadcasted_iota(jnp.int32, (rows, tk), 1)
                      < lax.broadcasted_iota(jnp.int32, (rows, tk), 0))
            log_om = jnp.where(causal, log_om, 0.0)
        within = _dot(log_om.astype(BF16), tri)
        w = jnp.exp2(log_beta + within + jnp.tile(carry_ref[r0:, :], (1, tk // LANES)))
        if masked:
            w = jnp.where(causal, w, 0.0)
        acc_ref[r0:, :] += _dot(w.astype(BF16), v)
        carry_ref[r0:, :] += jnp.sum(log_om, axis=-1, keepdims=True)

    acc_ref[...] = jnp.zeros_like(acc_ref)
    carry_ref[...] = jnp.zeros_like(carry_ref)
    for j in reversed(range(ratio)):
        block(i * ratio + j, j * tk, True)

    @pl.loop(0, i)
    def _(t):
        for j in reversed(range(ratio)):
            block((i - 1 - t) * ratio + j, 0, False)

    o_ref[0] = acc_ref[...].astype(o_ref.dtype)


def _sb_attention(qkv, *, n_heads, tq, tk):
    b, s, _ = qkv.shape
    return pl.pallas_call(
        functools.partial(_sb_attn_kernel, tq=tq, tk=tk),
        grid=(b, n_heads, s // tq),
        in_specs=[
            pl.BlockSpec((1, tq, HEAD_DIM), lambda bi, h, i: (bi, i, h)),
            pl.BlockSpec((1, s, HEAD_DIM), lambda bi, h, i: (bi, 0, n_heads + h)),
            pl.BlockSpec((1, s, HEAD_DIM), lambda bi, h, i: (bi, 0, 2 * n_heads + h)),
        ],
        out_specs=pl.BlockSpec((1, tq, HEAD_DIM), lambda bi, h, i: (bi, i, h)),
        out_shape=jax.ShapeDtypeStruct((b, s, n_heads * HEAD_DIM), BF16),
        scratch_shapes=[pltpu.VMEM((tq, HEAD_DIM), F32), pltpu.VMEM((tq, LANES), F32)],
        compiler_params=_params("parallel", "parallel", "arbitrary"),
        name="sb_attn",
    )(qkv, qkv, qkv)


def _diff_attn_kernel(q_ref, k_ref, v_ref, lq1_ref, lk1_ref, lq2_ref, lk2_ref, g_ref, o_ref,
                      qs_ref, m_ref, l_ref, acc_ref, *, tq, tk, lambda_init):
    i = pl.program_id(2)
    ratio = tq // tk
    q = q_ref[0]
    lane = lax.broadcasted_iota(jnp.int32, (tq, HEAD_DIM), 1)
    zero = jnp.zeros_like(q)
    qs_ref[0] = jnp.where(lane < DIFF_QK_DIM, q, zero)
    qs_ref[1] = jnp.where(lane >= DIFF_QK_DIM, q, zero)
    m_ref[...] = jnp.full_like(m_ref, -jnp.inf)
    l_ref[...] = jnp.zeros_like(l_ref)
    acc_ref[...] = jnp.zeros_like(acc_ref)

    def block(c, r0, masked):
        start = pl.multiple_of(c * tk, tk)
        k = k_ref[0, pl.ds(start, tk), :]
        v = v_ref[0, pl.ds(start, tk), :]
        for comp in range(2):
            sc = _dot_nt(qs_ref[comp, r0:, :], k)
            if masked:
                rows = tq - r0
                causal = (lax.broadcasted_iota(jnp.int32, (rows, tk), 1)
                          <= lax.broadcasted_iota(jnp.int32, (rows, tk), 0))
                sc = jnp.where(causal, sc, -jnp.inf)
            m_old = m_ref[comp, r0:, :]
            m_new = jnp.maximum(m_old, jnp.max(sc, axis=-1, keepdims=True))
            scale = jnp.exp2(m_old - m_new)
            pr = jnp.exp2(sc - jnp.tile(m_new, (1, tk // LANES)))
            m_ref[comp, r0:, :] = m_new
            l_ref[comp, r0:, :] = scale * l_ref[comp, r0:, :] + jnp.sum(pr, axis=-1, keepdims=True)
            acc_ref[comp, r0:, :] = scale * acc_ref[comp, r0:, :] + _dot(pr.astype(BF16), v)

    for j in reversed(range(ratio)):
        block(i * ratio + j, j * tk, True)

    @pl.loop(0, i)
    def _(t):
        for j in reversed(range(ratio)):
            block((i - 1 - t) * ratio + j, 0, False)

    lam = (jnp.exp(jnp.sum(lq1_ref[...] * lk1_ref[...], axis=-1, keepdims=True))
           - jnp.exp(jnp.sum(lq2_ref[...] * lk2_ref[...], axis=-1, keepdims=True))
           + lambda_init)
    o = acc_ref[0] / l_ref[0] - lam * (acc_ref[1] / l_ref[1])
    o = o * lax.rsqrt(jnp.mean(o * o, axis=-1, keepdims=True) + RMS_EPS)
    o_ref[0] = (o * g_ref[...] * (1.0 - lambda_init)).astype(o_ref.dtype)


def _diff_attention(qkv, lq1, lk1, lq2, lk2, subln_g, *, n_heads, col0, lambda_init, tq, tk):
    b, s, _ = qkv.shape
    small = lambda width: pl.BlockSpec((1, width), lambda bi, h, i: (0, 0))
    return pl.pallas_call(
        functools.partial(_diff_attn_kernel, tq=tq, tk=tk, lambda_init=lambda_init),
        grid=(b, n_heads, s // tq),
        in_specs=[
            pl.BlockSpec((1, tq, HEAD_DIM), lambda bi, h, i: (bi, i, col0 + h)),
            pl.BlockSpec((1, s, HEAD_DIM), lambda bi, h, i: (bi, 0, col0 + n_heads + h)),
            pl.BlockSpec((1, s, HEAD_DIM), lambda bi, h, i: (bi, 0, col0 + 2 * n_heads + h)),
            small(DIFF_QK_DIM), small(DIFF_QK_DIM), small(DIFF_QK_DIM), small(DIFF_QK_DIM),
            small(HEAD_DIM),
        ],
        out_specs=pl.BlockSpec((1, tq, HEAD_DIM), lambda bi, h, i: (bi, i, h)),
        out_shape=jax.ShapeDtypeStruct((b, s, n_heads * HEAD_DIM), BF16),
        scratch_shapes=[pltpu.VMEM((2, tq, HEAD_DIM), BF16), pltpu.VMEM((2, tq, LANES), F32),
                        pltpu.VMEM((2, tq, LANES), F32), pltpu.VMEM((2, tq, HEAD_DIM), F32)],
        compiler_params=_params("parallel", "parallel", "arbitrary"),
        name="diff_attn",
    )(qkv, qkv, qkv, lq1, lk1, lq2, lk2, subln_g)


def _outproj_ln_kernel(h_ref, osb_ref, odf_ref, wa_ref, wb_ref, g_ref, b_ref, o_ref, *, alpha):
    mix = _dot(osb_ref[...], wa_ref[...]) + _dot(odf_ref[...], wb_ref[...])
    o_ref[...] = _layernorm_rows(alpha * h_ref[...] + mix, g_ref[...], b_ref[...])


def _outproj_ln(h, o_sb, o_df, w_a, w_b, g, b, *, alpha, tm):
    n, d = h.shape
    wa_rows, wb_rows = w_a.shape[0], w_b.shape[0]
    return pl.pallas_call(
        functools.partial(_outproj_ln_kernel, alpha=alpha),
        grid=(n // tm,),
        in_specs=[
            pl.BlockSpec((tm, d), lambda i: (i, 0)),
            pl.BlockSpec((tm, wa_rows), lambda i: (i, 0)),
            pl.BlockSpec((tm, wb_rows), lambda i: (i, 0)),
            pl.BlockSpec((wa_rows, d), lambda i: (0, 0)),
            pl.BlockSpec((wb_rows, d), lambda i: (0, 0)),
            pl.BlockSpec((1, d), lambda i: (0, 0)),
            pl.BlockSpec((1, d), lambda i: (0, 0)),
        ],
        out_specs=pl.BlockSpec((tm, d), lambda i: (i, 0)),
        out_shape=jax.ShapeDtypeStruct((n, d), F32),
        compiler_params=_params("parallel"),
        name="outproj_ln",
    )(h, o_sb, o_df, w_a, w_b, g, b)


def _ple_ln_kernel(h_ref, p_ref, wg_ref, bg_ref, wp_ref, g_ref, b_ref, o_ref, *, alpha):
    h = h_ref[...]
    gate = jax.nn.sigmoid(_dot(h.astype(BF16), wg_ref[...]) + bg_ref[...])
    emb = _dot(p_ref[...].astype(BF16), wp_ref[...])
    o_ref[...] = _layernorm_rows(alpha * h + gate * emb, g_ref[...], b_ref[...])


def _ple_ln(h, p, wg, bg, wp, g, b, *, alpha, tm):
    n, d = h.shape
    dp = p.shape[1]
    return pl.pallas_call(
        functools.partial(_ple_ln_kernel, alpha=alpha),
        grid=(n // tm,),
        in_specs=[
            pl.BlockSpec((tm, d), lambda i: (i, 0)),
            pl.BlockSpec((tm, dp), lambda i: (i, 0)),
            pl.BlockSpec((d, d), lambda i: (0, 0)),
            pl.BlockSpec((1, d), lambda i: (0, 0)),
            pl.BlockSpec((dp, d), lambda i: (0, 0)),
            pl.BlockSpec((1, d), lambda i: (0, 0)),
            pl.BlockSpec((1, d), lambda i: (0, 0)),
        ],
        out_specs=pl.BlockSpec((tm, d), lambda i: (i, 0)),
        out_shape=jax.ShapeDtypeStruct((n, d), F32),
        compiler_params=_params("parallel"),
        name="ple_ln",
    )(h, p, wg, bg, wp, g, b)


def _tile(n, target):
    t = min(n, target)
    assert n % t == 0, (n, t)
    return t


def kernel(x, p, positions, ln_g, ln_b, ffn1_w_gate, ffn1_w_up, ffn1_w_down, w_in, w_out,
           lambda_q1, lambda_k1, lambda_q2, lambda_k2, diff_subln_g,
           ffn2_w_gate, ffn2_w_up, ffn2_w_down, ple_w_gate, ple_b_gate, ple_w_proj):
    batch, seq, d = x.shape
    depth = ln_g.shape[0]
    n = batch * seq
    n_sb = d // (2 * HEAD_DIM)
    n_diff = d // (2 * HEAD_DIM)
    sb_width = n_sb * HEAD_DIM
    alpha = (2.0 * depth) ** 0.25

    tm_ffn = _tile(n, 512)
    tf = _tile(ffn1_w_gate.shape[2], 256)
    tm_proj = _tile(n, 512)
    tq = _tile(seq, 1024)
    tk = _tile(tq, 256)

    inv_freq = ROPE_THETA ** (-jnp.arange(0, ROT_DIM, 2, dtype=F32) / ROT_DIM)
    lane = jnp.arange(LANES) % DIFF_QK_DIM
    freq = jnp.where(lane < ROT_DIM, inv_freq[lane % (ROT_DIM // 2)], 0.0).reshape(1, LANES)
    pos = positions.reshape(n, 1)

    row = lambda v: v.reshape(1, -1)
    h = x.reshape(n, d)
    for i in range(depth):
        lambda_init = 0.8 - 0.6 * math.exp(-0.3 * i)
        h = _ffn_ln(h, ffn1_w_gate[i].astype(BF16), ffn1_w_up[i].astype(BF16),
                    ffn1_w_down[i].astype(BF16), row(ln_g[i, 0]), row(ln_b[i, 0]),
                    alpha=alpha, tm=tm_ffn, tf=tf)
        qkv = _qkv_proj(h, pos, freq, w_in[i].astype(BF16), tm=tm_proj)
        qkv = qkv.reshape(batch, seq, -1)
        o_sb = _sb_attention(qkv, n_heads=n_sb, tq=tq, tk=tk)
        o_df = _diff_attention(qkv, row(lambda_q1[i]), row(lambda_k1[i]), row(lambda_q2[i]),
                               row(lambda_k2[i]), row(diff_subln_g[i]), n_heads=n_diff,
                               col0=3 * n_sb, lambda_init=lambda_init, tq=tq, tk=tk)
        w_o = w_out[i].astype(BF16)
        h = _outproj_ln(h, o_sb.reshape(n, -1), o_df.reshape(n, -1), w_o[:sb_width], w_o[sb_width:],
                        row(ln_g[i, 1]), row(ln_b[i, 1]), alpha=alpha, tm=tm_proj)
        h = _ffn_ln(h, ffn2_w_gate[i].astype(BF16), ffn2_w_up[i].astype(BF16),
                    ffn2_w_down[i].astype(BF16), row(ln_g[i, 2]), row(ln_b[i, 2]),
                    alpha=alpha, tm=tm_ffn, tf=tf)
        h = _ple_ln(h, p[i].reshape(n, -1), ple_w_gate[i].astype(BF16), row(ple_b_gate[i]),
                    ple_w_proj[i].astype(BF16), row(ln_g[i, 3]), row(ln_b[i, 3]),
                    alpha=alpha, tm=tm_proj)
    return h.reshape(batch, seq, d)
```

```python
import functools
import math

import jax
import jax.numpy as jnp
from jax import lax
from jax.experimental import pallas as pl
from jax.experimental.pallas import tpu as pltpu

HEAD_DIM = 128
DIFF_QK_DIM = HEAD_DIM // 2
ROT_DIM = DIFF_QK_DIM // 4
ROPE_THETA = 500000.0
LN_EPS = 1e-5
RMS_EPS = 1e-5

LANES = 128
VMEM_LIMIT_BYTES = 56 * 1024 * 1024

BF16 = jnp.bfloat16
F32 = jnp.float32

LOG2E = 1.4426950408889634

_NT = (((1,), (1,)), ((), ()))


def _dot(a, b):
    return jnp.dot(a, b, preferred_element_type=F32)


def _dot_nt(a, b):
    return lax.dot_general(a, b, _NT, preferred_element_type=F32)


def _layernorm_rows(y, g, b):
    mu = jnp.mean(y, axis=-1, keepdims=True)
    yc = y - mu
    var = jnp.mean(yc * yc, axis=-1, keepdims=True)
    return yc * lax.rsqrt(var + LN_EPS) * g + b


def _params(*semantics):
    return pltpu.CompilerParams(dimension_semantics=semantics,
                                vmem_limit_bytes=VMEM_LIMIT_BYTES)


def _ffn_ln_kernel(x_ref, wg_ref, wu_ref, wd_ref, g_ref, b_ref, o_ref, xb_ref, *, alpha):
    j = pl.program_id(1)

    @pl.when(j == 0)
    def _():
        xb_ref[...] = x_ref[...].astype(BF16)
        o_ref[...] = jnp.zeros_like(o_ref)

    xb = xb_ref[...]
    gate = _dot(xb, wg_ref[...])
    up = _dot(xb, wu_ref[...])
    act = (gate * jax.nn.sigmoid(gate) * up).astype(BF16)
    o_ref[...] += _dot(act, wd_ref[...])

    @pl.when(j == pl.num_programs(1) - 1)
    def _():
        y = alpha * x_ref[...] + 0.5 * o_ref[...]
        o_ref[...] = _layernorm_rows(y, g_ref[...], b_ref[...])


def _ffn_ln(h, wg, wu, wd, g, b, *, alpha, tm, tf):
    n, d = h.shape
    f = wg.shape[1]
    return pl.pallas_call(
        functools.partial(_ffn_ln_kernel, alpha=alpha),
        grid=(n // tm, f // tf),
        in_specs=[
            pl.BlockSpec((tm, d), lambda i, j: (i, 0)),
            pl.BlockSpec((d, tf), lambda i, j: (0, j)),
            pl.BlockSpec((d, tf), lambda i, j: (0, j)),
            pl.BlockSpec((tf, d), lambda i, j: (j, 0)),
            pl.BlockSpec((1, d), lambda i, j: (0, 0)),
            pl.BlockSpec((1, d), lambda i, j: (0, 0)),
        ],
        out_specs=pl.BlockSpec((tm, d), lambda i, j: (i, 0)),
        out_shape=jax.ShapeDtypeStruct((n, d), F32),
        scratch_shapes=[pltpu.VMEM((tm, d), BF16)],
        compiler_params=_params("parallel", "arbitrary"),
        name="ffn_ln",
    )(h, wg, wu, wd, g, b)


def _qkv_kernel(h_ref, pos_ref, freq_ref, w_ref, o_ref, hb_ref, c_ref, sa_ref, sb_ref,
                *, sb_scale, diff_scale):
    j = pl.program_id(1)

    @pl.when(j == 0)
    def _():
        hb_ref[...] = h_ref[...].astype(BF16)
        lane = lax.broadcasted_iota(jnp.int32, (1, LANES), 1) % DIFF_QK_DIM
        ang = pos_ref[...].astype(F32) * freq_ref[...]
        cos = jnp.cos(ang)
        sin = jnp.sin(ang)
        half = ROT_DIM // 2
        c_ref[...] = jnp.where(lane < ROT_DIM, cos, 1.0)
        sa_ref[...] = jnp.where(lane < half, -sin, 0.0)
        sb_ref[...] = jnp.where((lane >= half) & (lane < ROT_DIM), sin, 0.0)

    t = _dot(hb_ref[...], w_ref[...])
    tn = t.shape[1]

    def rope(t, scale):
        half = ROT_DIM // 2
        c, sa, sb = c_ref[...] * scale, sa_ref[...] * scale, sb_ref[...] * scale
        for hh in range(tn // LANES):
            th = t[:, hh * LANES:(hh + 1) * LANES]
            t_up = pltpu.roll(th, LANES - half, axis=1)
            t_dn = pltpu.roll(th, half, axis=1)
            o_ref[:, hh * LANES:(hh + 1) * LANES] = (th * c + t_up * sa + t_dn * sb).astype(o_ref.dtype)

    @pl.when(j == 0)
    def _():
        o_ref[...] = (t * sb_scale).astype(o_ref.dtype)

    @pl.when((j == 1) | (j == 2) | (j == 5))
    def _():
        o_ref[...] = t.astype(o_ref.dtype)

    @pl.when(j == 3)
    def _():
        rope(t, diff_scale)

    @pl.when(j == 4)
    def _():
        rope(t, 1.0)


def _qkv_proj(h, pos, freq, w_in, *, tm):
    n, d = h.shape
    width = w_in.shape[1]
    tn = width // 6
    return pl.pallas_call(
        functools.partial(_qkv_kernel, sb_scale=HEAD_DIM ** -0.5 * LOG2E,
                          diff_scale=DIFF_QK_DIM ** -0.5 * LOG2E),
        grid=(n // tm, 6),
        in_specs=[
            pl.BlockSpec((tm, d), lambda i, j: (i, 0)),
            pl.BlockSpec((tm, 1), lambda i, j: (i, 0)),
            pl.BlockSpec((1, LANES), lambda i, j: (0, 0)),
            pl.BlockSpec((d, tn), lambda i, j: (0, j)),
        ],
        out_specs=pl.BlockSpec((tm, tn), lambda i, j: (i, j)),
        out_shape=jax.ShapeDtypeStruct((n, width), BF16),
        scratch_shapes=[pltpu.VMEM((tm, d), BF16),
                        pltpu.VMEM((tm, LANES), F32),
                        pltpu.VMEM((tm, LANES), F32),
                        pltpu.VMEM((tm, LANES), F32)],
        compiler_params=_params("parallel", "arbitrary"),
        name="qkv_proj",
    )(h, pos, freq, w_in)


def _neg_abs(x):
    bits = lax.bitcast_convert_type(x, jnp.uint32) | jnp.uint32(0x80000000)
    return lax.bitcast_convert_type(bits, F32)


def _sb_attn_kernel(q_ref, k_ref, v_ref, o_ref, acc_ref, carry_ref, *, tq, tk):
    i = pl.program_id(2)
    ratio = tq // tk
    row = lax.broadcasted_iota(jnp.int32, (tk, tk), 0)
    col = lax.broadcasted_iota(jnp.int32, (tk, tk), 1)
    tri = jnp.where(row > col, 1.0, 0.0).astype(BF16)

    def block(c, r0, masked):
        start = pl.multiple_of(c * tk, tk)
        k = k_ref[0, pl.ds(start, tk), :]
        v = v_ref[0, pl.ds(start, tk), :]
        z = _dot_nt(q_ref[0, r0:, :], k)
        softplus = jnp.log(1.0 + jnp.exp2(_neg_abs(z))) * LOG2E
        log_beta = jnp.minimum(z, 0.0) - softplus
        log_om = log_beta - z
        if masked:
            rows = tq - r0
            causal = (lax.broadcasted_iota(jnp.int32, (rows, tk), 1)
                      < lax.broadcasted_iota(jnp.int32, (rows, tk), 0))
            log_om = jnp.where(causal, log_om, 0.0)
        within = _dot(log_om.astype(BF16), tri)
        w = jnp.exp2(log_beta + within + jnp.tile(carry_ref[r0:, :], (1, tk // LANES)))
        if masked:
            w = jnp.where(causal, w, 0.0)
        acc_ref[r0:, :] += _dot(w.astype(BF16), v)
        carry_ref[r0:, :] += within[:, :1] + log_om[:, :1]

    acc_ref[...] = jnp.zeros_like(acc_ref)
    carry_ref[...] = jnp.zeros_like(carry_ref)
    for j in reversed(range(ratio)):
        block(i * ratio + j, j * tk, True)

    @pl.loop(0, i)
    def _(t):
        for j in reversed(range(ratio)):
            block((i - 1 - t) * ratio + j, 0, False)

    o_ref[0] = acc_ref[...].astype(o_ref.dtype)


def _sb_attention(qkv, *, n_heads, tq, tk):
    b, s, _ = qkv.shape
    return pl.pallas_call(
        functools.partial(_sb_attn_kernel, tq=tq, tk=tk),
        grid=(b, n_heads, s // tq),
        in_specs=[
            pl.BlockSpec((1, tq, HEAD_DIM), lambda bi, h, i: (bi, i, h)),
            pl.BlockSpec((1, s, HEAD_DIM), lambda bi, h, i: (bi, 0, n_heads + h)),
            pl.BlockSpec((1, s, HEAD_DIM), lambda bi, h, i: (bi, 0, 2 * n_heads + h)),
        ],
        out_specs=pl.BlockSpec((1, tq, HEAD_DIM), lambda bi, h, i: (bi, i, h)),
        out_shape=jax.ShapeDtypeStruct((b, s, n_heads * HEAD_DIM), BF16),
        scratch_shapes=[pltpu.VMEM((tq, HEAD_DIM), F32), pltpu.VMEM((tq, LANES), F32)],
        compiler_params=_params("parallel", "parallel", "arbitrary"),
        name="sb_attn",
    )(qkv, qkv, qkv)


def _diff_attn_kernel(q_ref, k_ref, v_ref, lq1_ref, lk1_ref, lq2_ref, lk2_ref, g_ref, o_ref,
                      qs_ref, m_ref, acc_ref, *, tq, tk, lambda_init):
    i = pl.program_id(2)
    ratio = tq // tk
    q = q_ref[0]
    lane = lax.broadcasted_iota(jnp.int32, (tq, HEAD_DIM), 1)
    zero = jnp.zeros_like(q)
    qs_ref[0] = jnp.where(lane < DIFF_QK_DIM, q, zero)
    qs_ref[1] = jnp.where(lane >= DIFF_QK_DIM, q, zero)
    m_ref[...] = jnp.full_like(m_ref, -jnp.inf)
    acc_ref[...] = jnp.zeros_like(acc_ref)

    def block(c, r0, masked):
        start = pl.multiple_of(c * tk, tk)
        k = k_ref[0, pl.ds(start, tk), :]
        v_ones = jnp.concatenate([v_ref[0, pl.ds(start, tk), :], jnp.ones((tk, LANES), BF16)], axis=1)
        for comp in range(2):
            sc = _dot_nt(qs_ref[comp, r0:, :], k)
            if masked:
                rows = tq - r0
                causal = (lax.broadcasted_iota(jnp.int32, (rows, tk), 1)
                          <= lax.broadcasted_iota(jnp.int32, (rows, tk), 0))
                sc = jnp.where(causal, sc, -jnp.inf)
            m_old = m_ref[comp, r0:, :]
            m_new = jnp.maximum(m_old, jnp.max(sc, axis=-1, keepdims=True))
            scale = jnp.exp2(m_old - m_new)
            pr = jnp.exp2(sc - jnp.tile(m_new, (1, tk // LANES)))
            m_ref[comp, r0:, :] = m_new
            acc_ref[comp, r0:, :] = (jnp.tile(scale, (1, 2)) * acc_ref[comp, r0:, :]
                                     + _dot(pr.astype(BF16), v_ones))

    for j in reversed(range(ratio)):
        block(i * ratio + j, j * tk, True)

    @pl.loop(0, i)
    def _(t):
        for j in reversed(range(ratio)):
            block((i - 1 - t) * ratio + j, 0, False)

    lam = (jnp.exp(jnp.sum(lq1_ref[...] * lk1_ref[...], axis=-1, keepdims=True))
           - jnp.exp(jnp.sum(lq2_ref[...] * lk2_ref[...], axis=-1, keepdims=True))
           + lambda_init)
    o = (acc_ref[0, :, :HEAD_DIM] / acc_ref[0, :, HEAD_DIM:]
         - lam * (acc_ref[1, :, :HEAD_DIM] / acc_ref[1, :, HEAD_DIM:]))
    o = o * lax.rsqrt(jnp.mean(o * o, axis=-1, keepdims=True) + RMS_EPS)
    o_ref[0] = (o * g_ref[...] * (1.0 - lambda_init)).astype(o_ref.dtype)


def _diff_attention(qkv, lq1, lk1, lq2, lk2, subln_g, *, n_heads, col0, lambda_init, tq, tk):
    b, s, _ = qkv.shape
    small = lambda width: pl.BlockSpec((1, width), lambda bi, h, i: (0, 0))
    return pl.pallas_call(
        functools.partial(_diff_attn_kernel, tq=tq, tk=tk, lambda_init=lambda_init),
        grid=(b, n_heads, s // tq),
        in_specs=[
            pl.BlockSpec((1, tq, HEAD_DIM), lambda bi, h, i: (bi, i, col0 + h)),
            pl.BlockSpec((1, s, HEAD_DIM), lambda bi, h, i: (bi, 0, col0 + n_heads + h)),
            pl.BlockSpec((1, s, HEAD_DIM), lambda bi, h, i: (bi, 0, col0 + 2 * n_heads + h)),
            small(DIFF_QK_DIM), small(DIFF_QK_DIM), small(DIFF_QK_DIM), small(DIFF_QK_DIM),
            small(HEAD_DIM),
        ],
        out_specs=pl.BlockSpec((1, tq, HEAD_DIM), lambda bi, h, i: (bi, i, h)),
        out_shape=jax.ShapeDtypeStruct((b, s, n_heads * HEAD_DIM), BF16),
        scratch_shapes=[pltpu.VMEM((2, tq, HEAD_DIM), BF16), pltpu.VMEM((2, tq, LANES), F32),
                        pltpu.VMEM((2, tq, HEAD_DIM + LANES), F32)],
        compiler_params=_params("parallel", "parallel", "arbitrary"),
        name="diff_attn",
    )(qkv, qkv, qkv, lq1, lk1, lq2, lk2, subln_g)


def _outproj_ln_kernel(h_ref, osb_ref, odf_ref, wa_ref, wb_ref, g_ref, b_ref, o_ref, *, alpha):
    mix = _dot(osb_ref[...], wa_ref[...]) + _dot(odf_ref[...], wb_ref[...])
    o_ref[...] = _layernorm_rows(alpha * h_ref[...] + mix, g_ref[...], b_ref[...])


def _outproj_ln(h, o_sb, o_df, w_a, w_b, g, b, *, alpha, tm):
    n, d = h.shape
    wa_rows, wb_rows = w_a.shape[0], w_b.shape[0]
    return pl.pallas_call(
        functools.partial(_outproj_ln_kernel, alpha=alpha),
        grid=(n // tm,),
        in_specs=[
            pl.BlockSpec((tm, d), lambda i: (i, 0)),
            pl.BlockSpec((tm, wa_rows), lambda i: (i, 0)),
            pl.BlockSpec((tm, wb_rows), lambda i: (i, 0)),
            pl.BlockSpec((wa_rows, d), lambda i: (0, 0)),
            pl.BlockSpec((wb_rows, d), lambda i: (0, 0)),
            pl.BlockSpec((1, d), lambda i: (0, 0)),
            pl.BlockSpec((1, d), lambda i: (0, 0)),
        ],
        out_specs=pl.BlockSpec((tm, d), lambda i: (i, 0)),
        out_shape=jax.ShapeDtypeStruct((n, d), F32),
        compiler_params=_params("parallel"),
        name="outproj_ln",
    )(h, o_sb, o_df, w_a, w_b, g, b)


def _ple_ln_kernel(h_ref, p_ref, wg_ref, bg_ref, wp_ref, g_ref, b_ref, o_ref, *, alpha):
    h = h_ref[...]
    gate = jax.nn.sigmoid(_dot(h.astype(BF16), wg_ref[...]) + bg_ref[...])
    emb = _dot(p_ref[...].astype(BF16), wp_ref[...])
    o_ref[...] = _layernorm_rows(alpha * h + gate * emb, g_ref[...], b_ref[...])


def _ple_ln(h, p, wg, bg, wp, g, b, *, alpha, tm):
    n, d = h.shape
    dp = p.shape[1]
    return pl.pallas_call(
        functools.partial(_ple_ln_kernel, alpha=alpha),
        grid=(n // tm,),
        in_specs=[
            pl.BlockSpec((tm, d), lambda i: (i, 0)),
            pl.BlockSpec((tm, dp), lambda i: (i, 0)),
            pl.BlockSpec((d, d), lambda i: (0, 0)),
            pl.BlockSpec((1, d), lambda i: (0, 0)),
            pl.BlockSpec((dp, d), lambda i: (0, 0)),
            pl.BlockSpec((1, d), lambda i: (0, 0)),
            pl.BlockSpec((1, d), lambda i: (0, 0)),
        ],
        out_specs=pl.BlockSpec((tm, d), lambda i: (i, 0)),
        out_shape=jax.ShapeDtypeStruct((n, d), F32),
        compiler_params=_params("parallel"),
        name="ple_ln",
    )(h, p, wg, bg, wp, g, b)


def _tile(n, target):
    t = min(n, target)
    assert n % t == 0, (n, t)
    return t


def kernel(x, p, positions, ln_g, ln_b, ffn1_w_gate, ffn1_w_up, ffn1_w_down, w_in, w_out,
           lambda_q1, lambda_k1, lambda_q2, lambda_k2, diff_subln_g,
           ffn2_w_gate, ffn2_w_up, ffn2_w_down, ple_w_gate, ple_b_gate, ple_w_proj):
    batch, seq, d = x.shape
    depth = ln_g.shape[0]
    n = batch * seq
    n_sb = d // (2 * HEAD_DIM)
    n_diff = d // (2 * HEAD_DIM)
    sb_width = n_sb * HEAD_DIM
    alpha = (2.0 * depth) ** 0.25

    tm_ffn = _tile(n, 1024)
    tf = _tile(ffn1_w_gate.shape[2], 256)
    tm_proj = _tile(n, 1024)
    tm_row = _tile(n, 512)
    tq = _tile(seq, 2048)
    tk = _tile(tq, 256)

    inv_freq = ROPE_THETA ** (-jnp.arange(0, ROT_DIM, 2, dtype=F32) / ROT_DIM)
    lane = jnp.arange(LANES) % DIFF_QK_DIM
    freq = jnp.where(lane < ROT_DIM, inv_freq[lane % (ROT_DIM // 2)], 0.0).reshape(1, LANES)
    pos = positions.reshape(n, 1)

    row = lambda v: v.reshape(1, -1)
    h = x.reshape(n, d)
    for i in range(depth):
        lambda_init = 0.8 - 0.6 * math.exp(-0.3 * i)
        h = _ffn_ln(h, ffn1_w_gate[i].astype(BF16), ffn1_w_up[i].astype(BF16),
                    ffn1_w_down[i].astype(BF16), row(ln_g[i, 0]), row(ln_b[i, 0]),
                    alpha=alpha, tm=tm_ffn, tf=tf)
        qkv = _qkv_proj(h, pos, freq, w_in[i].astype(BF16), tm=tm_proj)
        qkv = qkv.reshape(batch, seq, -1)
        o_sb = _sb_attention(qkv, n_heads=n_sb, tq=tq, tk=tk)
        o_df = _diff_attention(qkv, row(lambda_q1[i]), row(lambda_k1[i]), row(lambda_q2[i]),
                               row(lambda_k2[i]), row(diff_subln_g[i]), n_heads=n_diff,
                               col0=3 * n_sb, lambda_init=lambda_init, tq=tq, tk=tk)
        w_o = w_out[i].astype(BF16)
        h = _outproj_ln(h, o_sb.reshape(n, -1), o_df.reshape(n, -1), w_o[:sb_width], w_o[sb_width:],
                        row(ln_g[i, 1]), row(ln_b[i, 1]), alpha=alpha, tm=tm_row)
        h = _ffn_ln(h, ffn2_w_gate[i].astype(BF16), ffn2_w_up[i].astype(BF16),
                    ffn2_w_down[i].astype(BF16), row(ln_g[i, 2]), row(ln_b[i, 2]),
                    alpha=alpha, tm=tm_ffn, tf=tf)
        h = _ple_ln(h, p[i].reshape(n, -1), ple_w_gate[i].astype(BF16), row(ple_b_gate[i]),
                    ple_w_proj[i].astype(BF16), row(ln_g[i, 3]), row(ln_b[i, 3]),
                    alpha=alpha, tm=tm_row)
    return h.reshape(batch, seq, d)
```

```python
import functools
import math

import jax
import jax.numpy as jnp
from jax import lax
from jax.experimental import pallas as pl
from jax.experimental.pallas import tpu as pltpu

HEAD_DIM = 128
DIFF_QK_DIM = HEAD_DIM // 2
ROT_DIM = DIFF_QK_DIM // 4
ROPE_THETA = 500000.0
LN_EPS = 1e-5
RMS_EPS = 1e-5

LANES = 128
VMEM_LIMIT_BYTES = 60 * 1024 * 1024

BF16 = jnp.bfloat16
F32 = jnp.float32

LOG2E = 1.4426950408889634

_NT = (((1,), (1,)), ((), ()))


def _dot(a, b):
    return jnp.dot(a, b, preferred_element_type=F32)


def _dot_nt(a, b):
    return lax.dot_general(a, b, _NT, preferred_element_type=F32)


def _layernorm_rows(y, g, b):
    mu = jnp.mean(y, axis=-1, keepdims=True)
    yc = y - mu
    var = jnp.mean(yc * yc, axis=-1, keepdims=True)
    return yc * lax.rsqrt(var + LN_EPS) * g + b


def _params(*semantics):
    return pltpu.CompilerParams(dimension_semantics=semantics,
                                vmem_limit_bytes=VMEM_LIMIT_BYTES)


def _ffn_ln_kernel(x_ref, wg_ref, wu_ref, wd_ref, g_ref, b_ref, o_ref, ob_ref, xb_ref, *, alpha):
    j = pl.program_id(1)

    @pl.when(j == 0)
    def _():
        xb_ref[...] = x_ref[...].astype(BF16)
        o_ref[...] = jnp.zeros_like(o_ref)

    xb = xb_ref[...]
    gate = _dot(xb, wg_ref[...])
    up = _dot(xb, wu_ref[...])
    act = (gate * jax.nn.sigmoid(gate) * up).astype(BF16)
    o_ref[...] += _dot(act, wd_ref[...])

    @pl.when(j == pl.num_programs(1) - 1)
    def _():
        y = alpha * x_ref[...] + 0.5 * o_ref[...]
        out = _layernorm_rows(y, g_ref[...], b_ref[...])
        o_ref[...] = out
        ob_ref[...] = out.astype(BF16)


def _ffn_ln(h, wg, wu, wd, g, b, *, alpha, tm, tf):
    n, d = h.shape
    f = wg.shape[1]
    return pl.pallas_call(
        functools.partial(_ffn_ln_kernel, alpha=alpha),
        grid=(n // tm, f // tf),
        in_specs=[
            pl.BlockSpec((tm, d), lambda i, j: (i, 0)),
            pl.BlockSpec((d, tf), lambda i, j: (0, j)),
            pl.BlockSpec((d, tf), lambda i, j: (0, j)),
            pl.BlockSpec((tf, d), lambda i, j: (j, 0)),
            pl.BlockSpec((1, d), lambda i, j: (0, 0)),
            pl.BlockSpec((1, d), lambda i, j: (0, 0)),
        ],
        out_specs=[pl.BlockSpec((tm, d), lambda i, j: (i, 0)),
                   pl.BlockSpec((tm, d), lambda i, j: (i, 0))],
        out_shape=[jax.ShapeDtypeStruct((n, d), F32), jax.ShapeDtypeStruct((n, d), BF16)],
        scratch_shapes=[pltpu.VMEM((tm, d), BF16)],
        compiler_params=_params("parallel", "arbitrary"),
        name="ffn_ln",
    )(h, wg, wu, wd, g, b)


def _qkv_kernel(h_ref, pos_ref, freq_ref, w_ref, o_ref, c_ref, sa_ref, sb_ref,
                *, sb_scale, diff_scale):
    j = pl.program_id(1)
    tn = o_ref.shape[1]
    half = ROT_DIM // 2

    def rope(scale):
        t = _dot(h_ref[...], w_ref[...])
        c, sa, sb = c_ref[...] * scale, sa_ref[...] * scale, sb_ref[...] * scale
        for hh in range(tn // LANES):
            th = t[:, hh * LANES:(hh + 1) * LANES]
            t_up = pltpu.roll(th, LANES - half, axis=1)
            t_dn = pltpu.roll(th, half, axis=1)
            o_ref[:, hh * LANES:(hh + 1) * LANES] = (th * c + t_up * sa + t_dn * sb).astype(o_ref.dtype)

    @pl.when((j != 3) & (j != 4))
    def _():
        scale = jnp.where(j == 0, sb_scale, 1.0).astype(F32)
        o_ref[...] = (_dot(h_ref[...], w_ref[...]) * scale).astype(o_ref.dtype)

    @pl.when(j == 3)
    def _():
        lane = lax.broadcasted_iota(jnp.int32, (1, LANES), 1) % DIFF_QK_DIM
        ang = pos_ref[...].astype(F32) * freq_ref[...]
        cos = jnp.cos(ang)
        sin = jnp.sin(ang)
        c_ref[...] = jnp.where(lane < ROT_DIM, cos, 1.0)
        sa_ref[...] = jnp.where(lane < half, -sin, 0.0)
        sb_ref[...] = jnp.where((lane >= half) & (lane < ROT_DIM), sin, 0.0)
        rope(diff_scale)

    @pl.when(j == 4)
    def _():
        rope(1.0)


def _qkv_proj(h, pos, freq, w_in, *, tm):
    n, d = h.shape
    width = w_in.shape[1]
    tn = width // 6
    return pl.pallas_call(
        functools.partial(_qkv_kernel, sb_scale=HEAD_DIM ** -0.5 * LOG2E,
                          diff_scale=DIFF_QK_DIM ** -0.5 * LOG2E),
        grid=(n // tm, 6),
        in_specs=[
            pl.BlockSpec((tm, d), lambda i, j: (i, 0)),
            pl.BlockSpec((tm, 1), lambda i, j: (i, 0)),
            pl.BlockSpec((1, LANES), lambda i, j: (0, 0)),
            pl.BlockSpec((d, tn), lambda i, j: (0, j)),
        ],
        out_specs=pl.BlockSpec((tm, tn), lambda i, j: (i, j)),
        out_shape=jax.ShapeDtypeStruct((n, width), BF16),
        scratch_shapes=[pltpu.VMEM((tm, LANES), F32),
                        pltpu.VMEM((tm, LANES), F32),
                        pltpu.VMEM((tm, LANES), F32)],
        compiler_params=_params("parallel", "arbitrary"),
        name="qkv_proj",
    )(h, pos, freq, w_in)


def _neg_abs16(x):
    bits = lax.bitcast_convert_type(x, jnp.uint16) | jnp.uint16(0x8000)
    return lax.bitcast_convert_type(bits, BF16)


def _sb_attn_kernel(q_ref, k_ref, v_ref, o_ref, acc_ref, carry_ref, *, tq, tk):
    i = pl.program_id(2)
    ratio = tq // tk
    row = lax.broadcasted_iota(jnp.int32, (tk, tk), 0)
    col = lax.broadcasted_iota(jnp.int32, (tk, tk), 1)
    tri = jnp.where(row > col, 1.0, 0.0).astype(BF16)

    def block(c, r0, masked):
        start = pl.multiple_of(c * tk, tk)
        k = k_ref[0, pl.ds(start, tk), :]
        v = v_ref[0, pl.ds(start, tk), :]
        z = _dot_nt(q_ref[0, r0:, :], k)
        zb = z.astype(BF16)
        softplus = jnp.log(1.0 + jnp.exp2(_neg_abs16(zb))) * LOG2E
        log_beta = jnp.minimum(zb, 0.0) - softplus
        log_om = log_beta - zb
        if masked:
            rows = tq - r0
            causal = (lax.broadcasted_iota(jnp.int32, (rows, tk), 1)
                      < lax.broadcasted_iota(jnp.int32, (rows, tk), 0))
            log_om = jnp.where(causal, log_om, jnp.zeros_like(log_om))
        within = _dot(log_om, tri)
        w = jnp.exp2(log_beta.astype(F32) + within + jnp.tile(carry_ref[r0:, :], (1, tk // LANES)))
        if masked:
            w = jnp.where(causal, w, 0.0)
        acc_ref[r0:, :] += _dot(w.astype(BF16), v)
        carry_ref[r0:, :] += within[:, :1] + log_om[:, :1].astype(F32)

    acc_ref[...] = jnp.zeros_like(acc_ref)
    carry_ref[...] = jnp.zeros_like(carry_ref)
    for j in reversed(range(ratio)):
        block(i * ratio + j, j * tk, True)

    @pl.loop(0, i)
    def _(t):
        for j in reversed(range(ratio)):
            block((i - 1 - t) * ratio + j, 0, False)

    o_ref[0] = acc_ref[...].astype(o_ref.dtype)


def _sb_attention(qkv, *, n_heads, tq, tk):
    b, s, _ = qkv.shape
    return pl.pallas_call(
        functools.partial(_sb_attn_kernel, tq=tq, tk=tk),
        grid=(b, n_heads, s // tq),
        in_specs=[
            pl.BlockSpec((1, tq, HEAD_DIM), lambda bi, h, i: (bi, i, h)),
            pl.BlockSpec((1, s, HEAD_DIM), lambda bi, h, i: (bi, 0, n_heads + h)),
            pl.BlockSpec((1, s, HEAD_DIM), lambda bi, h, i: (bi, 0, 2 * n_heads + h)),
        ],
        out_specs=pl.BlockSpec((1, tq, HEAD_DIM), lambda bi, h, i: (bi, i, h)),
        out_shape=jax.ShapeDtypeStruct((b, s, n_heads * HEAD_DIM), BF16),
        scratch_shapes=[pltpu.VMEM((tq, HEAD_DIM), F32), pltpu.VMEM((tq, LANES), F32)],
        compiler_params=_params("parallel", "parallel", "arbitrary"),
        name="sb_attn",
    )(qkv, qkv, qkv)


def _diff_attn_kernel(q_ref, k_ref, v_ref, lq1_ref, lk1_ref, lq2_ref, lk2_ref, g_ref, o_ref,
                      qs_ref, m_ref, acc_ref, *, tq, tk, lambda_init):
    i = pl.program_id(2)
    ratio = tq // tk
    q = q_ref[0]
    lane = lax.broadcasted_iota(jnp.int32, (tq, HEAD_DIM), 1)
    zero = jnp.zeros_like(q)
    qs_ref[0] = jnp.where(lane < DIFF_QK_DIM, q, zero)
    qs_ref[1] = jnp.where(lane >= DIFF_QK_DIM, q, zero)
    m_ref[...] = jnp.full_like(m_ref, -jnp.inf)
    acc_ref[...] = jnp.zeros_like(acc_ref)

    def block(c, r0, masked):
        start = pl.multiple_of(c * tk, tk)
        k = k_ref[0, pl.ds(start, tk), :]
        v_ones = jnp.concatenate([v_ref[0, pl.ds(start, tk), :], jnp.ones((tk, LANES), BF16)], axis=1)
        for comp in range(2):
            sc = _dot_nt(qs_ref[comp, r0:, :], k)
            if masked:
                rows = tq - r0
                causal = (lax.broadcasted_iota(jnp.int32, (rows, tk), 1)
                          <= lax.broadcasted_iota(jnp.int32, (rows, tk), 0))
                sc = jnp.where(causal, sc, -jnp.inf)
            m_old = m_ref[comp, r0:, :]
            m_new = jnp.maximum(m_old, jnp.max(sc, axis=-1, keepdims=True))
            scale = jnp.exp2(m_old - m_new)
            pr = jnp.exp2(sc - jnp.tile(m_new, (1, tk // LANES)))
            m_ref[comp, r0:, :] = m_new
            acc_ref[comp, r0:, :] = (jnp.tile(scale, (1, 2)) * acc_ref[comp, r0:, :]
                                     + _dot(pr.astype(BF16), v_ones))

    for j in reversed(range(ratio)):
        block(i * ratio + j, j * tk, True)

    @pl.loop(0, i)
    def _(t):
        for j in reversed(range(ratio)):
            block((i - 1 - t) * ratio + j, 0, False)

    lam = (jnp.exp(jnp.sum(lq1_ref[...] * lk1_ref[...], axis=-1, keepdims=True))
           - jnp.exp(jnp.sum(lq2_ref[...] * lk2_ref[...], axis=-1, keepdims=True))
           + lambda_init)
    o = (acc_ref[0, :, :HEAD_DIM] / acc_ref[0, :, HEAD_DIM:]
         - lam * (acc_ref[1, :, :HEAD_DIM] / acc_ref[1, :, HEAD_DIM:]))
    o = o * lax.rsqrt(jnp.mean(o * o, axis=-1, keepdims=True) + RMS_EPS)
    o_ref[0] = (o * g_ref[...] * (1.0 - lambda_init)).astype(o_ref.dtype)


def _diff_attention(qkv, lq1, lk1, lq2, lk2, subln_g, *, n_heads, col0, lambda_init, tq, tk):
    b, s, _ = qkv.shape
    small = lambda width: pl.BlockSpec((1, width), lambda bi, h, i: (0, 0))
    return pl.pallas_call(
        functools.partial(_diff_attn_kernel, tq=tq, tk=tk, lambda_init=lambda_init),
        grid=(b, n_heads, s // tq),
        in_specs=[
            pl.BlockSpec((1, tq, HEAD_DIM), lambda bi, h, i: (bi, i, col0 + h)),
            pl.BlockSpec((1, s, HEAD_DIM), lambda bi, h, i: (bi, 0, col0 + n_heads + h)),
            pl.BlockSpec((1, s, HEAD_DIM), lambda bi, h, i: (bi, 0, col0 + 2 * n_heads + h)),
            small(DIFF_QK_DIM), small(DIFF_QK_DIM), small(DIFF_QK_DIM), small(DIFF_QK_DIM),
            small(HEAD_DIM),
        ],
        out_specs=pl.BlockSpec((1, tq, HEAD_DIM), lambda bi, h, i: (bi, i, h)),
        out_shape=jax.ShapeDtypeStruct((b, s, n_heads * HEAD_DIM), BF16),
        scratch_shapes=[pltpu.VMEM((2, tq, HEAD_DIM), BF16), pltpu.VMEM((2, tq, LANES), F32),
                        pltpu.VMEM((2, tq, HEAD_DIM + LANES), F32)],
        compiler_params=_params("parallel", "parallel", "arbitrary"),
        name="diff_attn",
    )(qkv, qkv, qkv, lq1, lk1, lq2, lk2, subln_g)


def _outproj_ln_kernel(h_ref, osb_ref, odf_ref, wa_ref, wb_ref, g_ref, b_ref, o_ref, *, alpha):
    mix = _dot(osb_ref[...], wa_ref[...]) + _dot(odf_ref[...], wb_ref[...])
    o_ref[...] = _layernorm_rows(alpha * h_ref[...] + mix, g_ref[...], b_ref[...])


def _outproj_ln(h, o_sb, o_df, w_a, w_b, g, b, *, alpha, tm):
    n, d = h.shape
    wa_rows, wb_rows = w_a.shape[0], w_b.shape[0]
    return pl.pallas_call(
        functools.partial(_outproj_ln_kernel, alpha=alpha),
        grid=(n // tm,),
        in_specs=[
            pl.BlockSpec((tm, d), lambda i: (i, 0)),
            pl.BlockSpec((tm, wa_rows), lambda i: (i, 0)),
            pl.BlockSpec((tm, wb_rows), lambda i: (i, 0)),
            pl.BlockSpec((wa_rows, d), lambda i: (0, 0)),
            pl.BlockSpec((wb_rows, d), lambda i: (0, 0)),
            pl.BlockSpec((1, d), lambda i: (0, 0)),
            pl.BlockSpec((1, d), lambda i: (0, 0)),
        ],
        out_specs=pl.BlockSpec((tm, d), lambda i: (i, 0)),
        out_shape=jax.ShapeDtypeStruct((n, d), F32),
        compiler_params=_params("parallel"),
        name="outproj_ln",
    )(h, o_sb, o_df, w_a, w_b, g, b)


def _ple_ln_kernel(h_ref, hb_ref, p_ref, wg_ref, bg_ref, wp_ref, g_ref, b_ref, o_ref, *, alpha):
    gate = jax.nn.sigmoid(_dot(hb_ref[...], wg_ref[...]) + bg_ref[...])
    emb = _dot(p_ref[...].astype(BF16), wp_ref[...])
    o_ref[...] = _layernorm_rows(alpha * h_ref[...] + gate * emb, g_ref[...], b_ref[...])


def _ple_ln(h, hb, p, wg, bg, wp, g, b, *, alpha, tm):
    n, d = h.shape
    dp = p.shape[1]
    return pl.pallas_call(
        functools.partial(_ple_ln_kernel, alpha=alpha),
        grid=(n // tm,),
        in_specs=[
            pl.BlockSpec((tm, d), lambda i: (i, 0)),
            pl.BlockSpec((tm, d), lambda i: (i, 0)),
            pl.BlockSpec((tm, dp), lambda i: (i, 0)),
            pl.BlockSpec((d, d), lambda i: (0, 0)),
            pl.BlockSpec((1, d), lambda i: (0, 0)),
            pl.BlockSpec((dp, d), lambda i: (0, 0)),
            pl.BlockSpec((1, d), lambda i: (0, 0)),
            pl.BlockSpec((1, d), lambda i: (0, 0)),
        ],
        out_specs=pl.BlockSpec((tm, d), lambda i: (i, 0)),
        out_shape=jax.ShapeDtypeStruct((n, d), F32),
        compiler_params=_params("parallel"),
        name="ple_ln",
    )(h, hb, p, wg, bg, wp, g, b)


def _tile(n, target):
    t = min(n, target)
    assert n % t == 0, (n, t)
    return t


def kernel(x, p, positions, ln_g, ln_b, ffn1_w_gate, ffn1_w_up, ffn1_w_down, w_in, w_out,
           lambda_q1, lambda_k1, lambda_q2, lambda_k2, diff_subln_g,
           ffn2_w_gate, ffn2_w_up, ffn2_w_down, ple_w_gate, ple_b_gate, ple_w_proj):
    batch, seq, d = x.shape
    depth = ln_g.shape[0]
    n = batch * seq
    n_sb = d // (2 * HEAD_DIM)
    n_diff = d // (2 * HEAD_DIM)
    sb_width = n_sb * HEAD_DIM
    alpha = (2.0 * depth) ** 0.25

    tm_ffn = _tile(n, 1024)
    tf = _tile(ffn1_w_gate.shape[2], 256)
    tm_proj = _tile(n, 1024)
    tm_row = _tile(n, 512)
    tq = _tile(seq, 2048)
    tk = _tile(tq, 256)

    inv_freq = ROPE_THETA ** (-jnp.arange(0, ROT_DIM, 2, dtype=F32) / ROT_DIM)
    lane = jnp.arange(LANES) % DIFF_QK_DIM
    freq = jnp.where(lane < ROT_DIM, inv_freq[lane % (ROT_DIM // 2)], 0.0).reshape(1, LANES)
    pos = positions.reshape(n, 1)

    row = lambda v: v.reshape(1, -1)
    h = x.reshape(n, d)
    for i in range(depth):
        lambda_init = 0.8 - 0.6 * math.exp(-0.3 * i)
        h, hb = _ffn_ln(h, ffn1_w_gate[i].astype(BF16), ffn1_w_up[i].astype(BF16),
                        ffn1_w_down[i].astype(BF16), row(ln_g[i, 0]), row(ln_b[i, 0]),
                        alpha=alpha, tm=tm_ffn, tf=tf)
        qkv = _qkv_proj(hb, pos, freq, w_in[i].astype(BF16), tm=tm_proj)
        qkv = qkv.reshape(batch, seq, -1)
        o_sb = _sb_attention(qkv, n_heads=n_sb, tq=tq, tk=tk)
        o_df = _diff_attention(qkv, row(lambda_q1[i]), row(lambda_k1[i]), row(lambda_q2[i]),
                               row(lambda_k2[i]), row(diff_subln_g[i]), n_heads=n_diff,
                               col0=3 * n_sb, lambda_init=lambda_init, tq=tq, tk=tk)
        w_o = w_out[i].astype(BF16)
        h = _outproj_ln(h, o_sb.reshape(n, -1), o_df.reshape(n, -1), w_o[:sb_width], w_o[sb_width:],
                        row(ln_g[i, 1]), row(ln_b[i, 1]), alpha=alpha, tm=tm_row)
        h, hb = _ffn_ln(h, ffn2_w_gate[i].astype(BF16), ffn2_w_up[i].astype(BF16),
                        ffn2_w_down[i].astype(BF16), row(ln_g[i, 2]), row(ln_b[i, 2]),
                        alpha=alpha, tm=tm_ffn, tf=tf)
        h = _ple_ln(h, hb, p[i].reshape(n, -1), ple_w_gate[i].astype(BF16), row(ple_b_gate[i]),
                    ple_w_proj[i].astype(BF16), row(ln_g[i, 3]), row(ln_b[i, 3]),
                    alpha=alpha, tm=tm_row)
    return h.reshape(batch, seq, d)
```

```python
import functools
import math

import jax
import jax.numpy as jnp
from jax import lax
from jax.experimental import pallas as pl
from jax.experimental.pallas import tpu as pltpu

HEAD_DIM = 128
DIFF_QK_DIM = HEAD_DIM // 2
ROT_DIM = DIFF_QK_DIM // 4
ROPE_THETA = 500000.0
LN_EPS = 1e-5
RMS_EPS = 1e-5

LANES = 128
VMEM_LIMIT_BYTES = 60 * 1024 * 1024

BF16 = jnp.bfloat16
F32 = jnp.float32

LOG2E = 1.4426950408889634

_NT = (((1,), (1,)), ((), ()))


def _dot(a, b):
    return jnp.dot(a, b, preferred_element_type=F32)


def _dot_nt(a, b):
    return lax.dot_general(a, b, _NT, preferred_element_type=F32)


def _layernorm_rows(y, g, b):
    mu = jnp.mean(y, axis=-1, keepdims=True)
    yc = y - mu
    var = jnp.mean(yc * yc, axis=-1, keepdims=True)
    return yc * lax.rsqrt(var + LN_EPS) * g + b


def _params(*semantics):
    return pltpu.CompilerParams(dimension_semantics=semantics,
                                vmem_limit_bytes=VMEM_LIMIT_BYTES)


def _ffn_ln_kernel(x_ref, wg_ref, wu_ref, wd_ref, g_ref, b_ref, o_ref, ob_ref, xb_ref, *, alpha):
    j = pl.program_id(1)

    @pl.when(j == 0)
    def _():
        xb_ref[...] = x_ref[...].astype(BF16)
        o_ref[...] = jnp.zeros_like(o_ref)

    xb = xb_ref[...]
    gate = _dot(xb, wg_ref[...])
    up = _dot(xb, wu_ref[...])
    act = (gate * jax.nn.sigmoid(gate) * up).astype(BF16)
    o_ref[...] += _dot(act, wd_ref[...])

    @pl.when(j == pl.num_programs(1) - 1)
    def _():
        y = alpha * x_ref[...] + 0.5 * o_ref[...]
        out = _layernorm_rows(y, g_ref[...], b_ref[...])
        o_ref[...] = out
        ob_ref[...] = out.astype(BF16)


def _ffn_ln(h, wg, wu, wd, g, b, *, alpha, tm, tf):
    n, d = h.shape
    f = wg.shape[1]
    return pl.pallas_call(
        functools.partial(_ffn_ln_kernel, alpha=alpha),
        grid=(n // tm, f // tf),
        in_specs=[
            pl.BlockSpec((tm, d), lambda i, j: (i, 0)),
            pl.BlockSpec((d, tf), lambda i, j: (0, j)),
            pl.BlockSpec((d, tf), lambda i, j: (0, j)),
            pl.BlockSpec((tf, d), lambda i, j: (j, 0)),
            pl.BlockSpec((1, d), lambda i, j: (0, 0)),
            pl.BlockSpec((1, d), lambda i, j: (0, 0)),
        ],
        out_specs=[pl.BlockSpec((tm, d), lambda i, j: (i, 0)),
                   pl.BlockSpec((tm, d), lambda i, j: (i, 0))],
        out_shape=[jax.ShapeDtypeStruct((n, d), F32), jax.ShapeDtypeStruct((n, d), BF16)],
        scratch_shapes=[pltpu.VMEM((tm, d), BF16)],
        compiler_params=_params("parallel", "arbitrary"),
        name="ffn_ln",
    )(h, wg, wu, wd, g, b)


def _qkv_kernel(h_ref, pos_ref, freq_ref, w_ref, o_ref, c_ref, sa_ref, sb_ref,
                *, sb_scale, diff_scale):
    j = pl.program_id(1)
    tn = o_ref.shape[1]
    half = ROT_DIM // 2

    def rope(scale):
        t = _dot(h_ref[...], w_ref[...])
        c, sa, sb = c_ref[...] * scale, sa_ref[...] * scale, sb_ref[...] * scale
        for hh in range(tn // LANES):
            th = t[:, hh * LANES:(hh + 1) * LANES]
            t_up = pltpu.roll(th, LANES - half, axis=1)
            t_dn = pltpu.roll(th, half, axis=1)
            o_ref[:, hh * LANES:(hh + 1) * LANES] = (th * c + t_up * sa + t_dn * sb).astype(o_ref.dtype)

    @pl.when((j != 3) & (j != 4))
    def _():
        scale = jnp.where(j == 0, sb_scale, 1.0).astype(F32)
        o_ref[...] = (_dot(h_ref[...], w_ref[...]) * scale).astype(o_ref.dtype)

    @pl.when(j == 3)
    def _():
        lane = lax.broadcasted_iota(jnp.int32, (1, LANES), 1) % DIFF_QK_DIM
        ang = pos_ref[...].astype(F32) * freq_ref[...]
        cos = jnp.cos(ang)
        sin = jnp.sin(ang)
        c_ref[...] = jnp.where(lane < ROT_DIM, cos, 1.0)
        sa_ref[...] = jnp.where(lane < half, -sin, 0.0)
        sb_ref[...] = jnp.where((lane >= half) & (lane < ROT_DIM), sin, 0.0)
        rope(diff_scale)

    @pl.when(j == 4)
    def _():
        rope(1.0)


def _qkv_proj(h, pos, freq, w_in, *, tm):
    n, d = h.shape
    width = w_in.shape[1]
    tn = width // 6
    return pl.pallas_call(
        functools.partial(_qkv_kernel, sb_scale=HEAD_DIM ** -0.5 * LOG2E,
                          diff_scale=DIFF_QK_DIM ** -0.5 * LOG2E),
        grid=(n // tm, 6),
        in_specs=[
            pl.BlockSpec((tm, d), lambda i, j: (i, 0)),
            pl.BlockSpec((tm, 1), lambda i, j: (i, 0)),
            pl.BlockSpec((1, LANES), lambda i, j: (0, 0)),
            pl.BlockSpec((d, tn), lambda i, j: (0, j)),
        ],
        out_specs=pl.BlockSpec((tm, tn), lambda i, j: (i, j)),
        out_shape=jax.ShapeDtypeStruct((n, width), BF16),
        scratch_shapes=[pltpu.VMEM((tm, LANES), F32),
                        pltpu.VMEM((tm, LANES), F32),
                        pltpu.VMEM((tm, LANES), F32)],
        compiler_params=_params("parallel", "arbitrary"),
        name="qkv_proj",
    )(h, pos, freq, w_in)


def _neg_abs16(x):
    bits = lax.bitcast_convert_type(x, jnp.uint16) | jnp.uint16(0x8000)
    return lax.bitcast_convert_type(bits, BF16)


def _sb_attn_kernel(q_ref, k_ref, v_ref, o_ref, acc_ref, carry_ref, *, tq, tk):
    i = pl.program_id(2)
    ratio = tq // tk
    row = lax.broadcasted_iota(jnp.int32, (tk, tk), 0)
    col = lax.broadcasted_iota(jnp.int32, (tk, tk), 1)
    tri = jnp.where(row > col, 1.0, 0.0).astype(BF16)

    def scores(c, r0, masked):
        k = k_ref[0, pl.ds(pl.multiple_of(c * tk, tk), tk), :]
        zb = _dot_nt(q_ref[0, r0:, :], k).astype(BF16)
        softplus = jnp.log(1.0 + jnp.exp2(_neg_abs16(zb))) * LOG2E
        log_beta = jnp.minimum(zb, 0.0) - softplus
        log_om = log_beta - zb
        causal = None
        if masked:
            rows = tq - r0
            causal = (lax.broadcasted_iota(jnp.int32, (rows, tk), 1)
                      < lax.broadcasted_iota(jnp.int32, (rows, tk), 0))
            log_om = jnp.where(causal, log_om, jnp.zeros_like(log_om))
        within = _dot(log_om, tri)
        return log_beta, within, within[:, :1] + log_om[:, :1].astype(F32), causal

    def accumulate(c, r0, log_beta, within, total, causal):
        v = v_ref[0, pl.ds(pl.multiple_of(c * tk, tk), tk), :]
        w = jnp.exp2(log_beta.astype(F32) + within + jnp.tile(carry_ref[r0:, :], (1, tk // LANES)))
        if causal is not None:
            w = jnp.where(causal, w, 0.0)
        acc_ref[r0:, :] += _dot(w.astype(BF16), v)
        carry_ref[r0:, :] += total

    def run(blocks):
        staged = [scores(c, r0, masked) for c, r0, masked in blocks]
        for (c, r0, _), st in zip(blocks, staged):
            accumulate(c, r0, *st)

    acc_ref[...] = jnp.zeros_like(acc_ref)
    carry_ref[...] = jnp.zeros_like(carry_ref)
    run([(i * ratio + j, j * tk, True) for j in reversed(range(ratio))])

    @pl.loop(0, i)
    def _(t):
        run([((i - 1 - t) * ratio + j, 0, False) for j in reversed(range(ratio))])

    o_ref[0] = acc_ref[...].astype(o_ref.dtype)


def _sb_attention(qkv, *, n_heads, tq, tk):
    b, s, _ = qkv.shape
    return pl.pallas_call(
        functools.partial(_sb_attn_kernel, tq=tq, tk=tk),
        grid=(b, n_heads, s // tq),
        in_specs=[
            pl.BlockSpec((1, tq, HEAD_DIM), lambda bi, h, i: (bi, i, h)),
            pl.BlockSpec((1, s, HEAD_DIM), lambda bi, h, i: (bi, 0, n_heads + h)),
            pl.BlockSpec((1, s, HEAD_DIM), lambda bi, h, i: (bi, 0, 2 * n_heads + h)),
        ],
        out_specs=pl.BlockSpec((1, tq, HEAD_DIM), lambda bi, h, i: (bi, i, h)),
        out_shape=jax.ShapeDtypeStruct((b, s, n_heads * HEAD_DIM), BF16),
        scratch_shapes=[pltpu.VMEM((tq, HEAD_DIM), F32), pltpu.VMEM((tq, LANES), F32)],
        compiler_params=_params("parallel", "parallel", "arbitrary"),
        name="sb_attn",
    )(qkv, qkv, qkv)


def _diff_attn_kernel(q_ref, k_ref, v_ref, lq1_ref, lk1_ref, lq2_ref, lk2_ref, g_ref, o_ref,
                      qs_ref, m_ref, acc_ref, *, tq, tk, lambda_init):
    i = pl.program_id(2)
    ratio = tq // tk
    q = q_ref[0]
    lane = lax.broadcasted_iota(jnp.int32, (tq, HEAD_DIM), 1)
    zero = jnp.zeros_like(q)
    qs_ref[0] = jnp.where(lane < DIFF_QK_DIM, q, zero)
    qs_ref[1] = jnp.where(lane >= DIFF_QK_DIM, q, zero)
    m_ref[...] = jnp.full_like(m_ref, -jnp.inf)
    acc_ref[...] = jnp.zeros_like(acc_ref)

    def block(c, r0, masked):
        start = pl.multiple_of(c * tk, tk)
        k = k_ref[0, pl.ds(start, tk), :]
        v_ones = jnp.concatenate([v_ref[0, pl.ds(start, tk), :], jnp.ones((tk, LANES), BF16)], axis=1)
        for comp in range(2):
            sc = _dot_nt(qs_ref[comp, r0:, :], k)
            if masked:
                rows = tq - r0
                causal = (lax.broadcasted_iota(jnp.int32, (rows, tk), 1)
                          <= lax.broadcasted_iota(jnp.int32, (rows, tk), 0))
                sc = jnp.where(causal, sc, -jnp.inf)
            m_old = m_ref[comp, r0:, :]
            m_new = jnp.maximum(m_old, jnp.max(sc, axis=-1, keepdims=True))
            scale = jnp.exp2(m_old - m_new)
            pr = jnp.exp2(sc - jnp.tile(m_new, (1, tk // LANES)))
            m_ref[comp, r0:, :] = m_new
            acc_ref[comp, r0:, :] = (jnp.tile(scale, (1, 2)) * acc_ref[comp, r0:, :]
                                     + _dot(pr.astype(BF16), v_ones))

    for j in reversed(range(ratio)):
        block(i * ratio + j, j * tk, True)

    @pl.loop(0, i)
    def _(t):
        for j in reversed(range(ratio)):
            block((i - 1 - t) * ratio + j, 0, False)

    lam = (jnp.exp(jnp.sum(lq1_ref[...] * lk1_ref[...], axis=-1, keepdims=True))
           - jnp.exp(jnp.sum(lq2_ref[...] * lk2_ref[...], axis=-1, keepdims=True))
           + lambda_init)
    o = (acc_ref[0, :, :HEAD_DIM] / acc_ref[0, :, HEAD_DIM:]
         - lam * (acc_ref[1, :, :HEAD_DIM] / acc_ref[1, :, HEAD_DIM:]))
    o = o * lax.rsqrt(jnp.mean(o * o, axis=-1, keepdims=True) + RMS_EPS)
    o_ref[0] = (o * g_ref[...] * (1.0 - lambda_init)).astype(o_ref.dtype)


def _diff_attention(qkv, lq1, lk1, lq2, lk2, subln_g, *, n_heads, col0, lambda_init, tq, tk):
    b, s, _ = qkv.shape
    small = lambda width: pl.BlockSpec((1, width), lambda bi, h, i: (0, 0))
    return pl.pallas_call(
        functools.partial(_diff_attn_kernel, tq=tq, tk=tk, lambda_init=lambda_init),
        grid=(b, n_heads, s // tq),
        in_specs=[
            pl.BlockSpec((1, tq, HEAD_DIM), lambda bi, h, i: (bi, i, col0 + h)),
            pl.BlockSpec((1, s, HEAD_DIM), lambda bi, h, i: (bi, 0, col0 + n_heads + h)),
            pl.BlockSpec((1, s, HEAD_DIM), lambda bi, h, i: (bi, 0, col0 + 2 * n_heads + h)),
            small(DIFF_QK_DIM), small(DIFF_QK_DIM), small(DIFF_QK_DIM), small(DIFF_QK_DIM),
            small(HEAD_DIM),
        ],
        out_specs=pl.BlockSpec((1, tq, HEAD_DIM), lambda bi, h, i: (bi, i, h)),
        out_shape=jax.ShapeDtypeStruct((b, s, n_heads * HEAD_DIM), BF16),
        scratch_shapes=[pltpu.VMEM((2, tq, HEAD_DIM), BF16), pltpu.VMEM((2, tq, LANES), F32),
                        pltpu.VMEM((2, tq, HEAD_DIM + LANES), F32)],
        compiler_params=_params("parallel", "parallel", "arbitrary"),
        name="diff_attn",
    )(qkv, qkv, qkv, lq1, lk1, lq2, lk2, subln_g)


def _outproj_ln_kernel(h_ref, osb_ref, odf_ref, wa_ref, wb_ref, g_ref, b_ref, o_ref, *, alpha):
    mix = _dot(osb_ref[...], wa_ref[...]) + _dot(odf_ref[...], wb_ref[...])
    o_ref[...] = _layernorm_rows(alpha * h_ref[...] + mix, g_ref[...], b_ref[...])


def _outproj_ln(h, o_sb, o_df, w_a, w_b, g, b, *, alpha, tm):
    n, d = h.shape
    wa_rows, wb_rows = w_a.shape[0], w_b.shape[0]
    return pl.pallas_call(
        functools.partial(_outproj_ln_kernel, alpha=alpha),
        grid=(n // tm,),
        in_specs=[
            pl.BlockSpec((tm, d), lambda i: (i, 0)),
            pl.BlockSpec((tm, wa_rows), lambda i: (i, 0)),
            pl.BlockSpec((tm, wb_rows), lambda i: (i, 0)),
            pl.BlockSpec((wa_rows, d), lambda i: (0, 0)),
            pl.BlockSpec((wb_rows, d), lambda i: (0, 0)),
            pl.BlockSpec((1, d), lambda i: (0, 0)),
            pl.BlockSpec((1, d), lambda i: (0, 0)),
        ],
        out_specs=pl.BlockSpec((tm, d), lambda i: (i, 0)),
        out_shape=jax.ShapeDtypeStruct((n, d), F32),
        compiler_params=_params("parallel"),
        name="outproj_ln",
    )(h, o_sb, o_df, w_a, w_b, g, b)


def _ple_ln_kernel(h_ref, hb_ref, p_ref, wg_ref, bg_ref, wp_ref, g_ref, b_ref, o_ref, *, alpha):
    gate = jax.nn.sigmoid(_dot(hb_ref[...], wg_ref[...]) + bg_ref[...])
    emb = _dot(p_ref[...].astype(BF16), wp_ref[...])
    o_ref[...] = _layernorm_rows(alpha * h_ref[...] + gate * emb, g_ref[...], b_ref[...])


def _ple_ln(h, hb, p, wg, bg, wp, g, b, *, alpha, tm):
    n, d = h.shape
    dp = p.shape[1]
    return pl.pallas_call(
        functools.partial(_ple_ln_kernel, alpha=alpha),
        grid=(n // tm,),
        in_specs=[
            pl.BlockSpec((tm, d), lambda i: (i, 0)),
            pl.BlockSpec((tm, d), lambda i: (i, 0)),
            pl.BlockSpec((tm, dp), lambda i: (i, 0)),
            pl.BlockSpec((d, d), lambda i: (0, 0)),
            pl.BlockSpec((1, d), lambda i: (0, 0)),
            pl.BlockSpec((dp, d), lambda i: (0, 0)),
            pl.BlockSpec((1, d), lambda i: (0, 0)),
            pl.BlockSpec((1, d), lambda i: (0, 0)),
        ],
        out_specs=pl.BlockSpec((tm, d), lambda i: (i, 0)),
        out_shape=jax.ShapeDtypeStruct((n, d), F32),
        compiler_params=_params("parallel"),
        name="ple_ln",
    )(h, hb, p, wg, bg, wp, g, b)


def _tile(n, target):
    t = min(n, target)
    assert n % t == 0, (n, t)
    return t


def kernel(x, p, positions, ln_g, ln_b, ffn1_w_gate, ffn1_w_up, ffn1_w_down, w_in, w_out,
           lambda_q1, lambda_k1, lambda_q2, lambda_k2, diff_subln_g,
           ffn2_w_gate, ffn2_w_up, ffn2_w_down, ple_w_gate, ple_b_gate, ple_w_proj):
    batch, seq, d = x.shape
    depth = ln_g.shape[0]
    n = batch * seq
    n_sb = d // (2 * HEAD_DIM)
    n_diff = d // (2 * HEAD_DIM)
    sb_width = n_sb * HEAD_DIM
    alpha = (2.0 * depth) ** 0.25

    tm_ffn = _tile(n, 1024)
    tf = _tile(ffn1_w_gate.shape[2], 256)
    tm_proj = _tile(n, 1024)
    tm_row = _tile(n, 512)
    tq = _tile(seq, 2048)
    tk = _tile(tq, 256)

    inv_freq = ROPE_THETA ** (-jnp.arange(0, ROT_DIM, 2, dtype=F32) / ROT_DIM)
    lane = jnp.arange(LANES) % DIFF_QK_DIM
    freq = jnp.where(lane < ROT_DIM, inv_freq[lane % (ROT_DIM // 2)], 0.0).reshape(1, LANES)
    pos = positions.reshape(n, 1)

    row = lambda v: v.reshape(1, -1)
    h = x.reshape(n, d)
    for i in range(depth):
        lambda_init = 0.8 - 0.6 * math.exp(-0.3 * i)
        h, hb = _ffn_ln(h, ffn1_w_gate[i].astype(BF16), ffn1_w_up[i].astype(BF16),
                        ffn1_w_down[i].astype(BF16), row(ln_g[i, 0]), row(ln_b[i, 0]),
                        alpha=alpha, tm=tm_ffn, tf=tf)
        qkv = _qkv_proj(hb, pos, freq, w_in[i].astype(BF16), tm=tm_proj)
        qkv = qkv.reshape(batch, seq, -1)
        o_sb = _sb_attention(qkv, n_heads=n_sb, tq=tq, tk=tk)
        o_df = _diff_attention(qkv, row(lambda_q1[i]), row(lambda_k1[i]), row(lambda_q2[i]),
                               row(lambda_k2[i]), row(diff_subln_g[i]), n_heads=n_diff,
                               col0=3 * n_sb, lambda_init=lambda_init, tq=tq, tk=tk)
        w_o = w_out[i].astype(BF16)
        h = _outproj_ln(h, o_sb.reshape(n, -1), o_df.reshape(n, -1), w_o[:sb_width], w_o[sb_width:],
                        row(ln_g[i, 1]), row(ln_b[i, 1]), alpha=alpha, tm=tm_row)
        h, hb = _ffn_ln(h, ffn2_w_gate[i].astype(BF16), ffn2_w_up[i].astype(BF16),
                        ffn2_w_down[i].astype(BF16), row(ln_g[i, 2]), row(ln_b[i, 2]),
                        alpha=alpha, tm=tm_ffn, tf=tf)
        h = _ple_ln(h, hb, p[i].reshape(n, -1), ple_w_gate[i].astype(BF16), row(ple_b_gate[i]),
                    ple_w_proj[i].astype(BF16), row(ln_g[i, 3]), row(ln_b[i, 3]),
                    alpha=alpha, tm=tm_row)
    return h.reshape(batch, seq, d)
```

```python
import functools
import math

import jax
import jax.numpy as jnp
from jax import lax
from jax.experimental import pallas as pl
from jax.experimental.pallas import tpu as pltpu

HEAD_DIM = 128
DIFF_QK_DIM = HEAD_DIM // 2
ROT_DIM = DIFF_QK_DIM // 4
ROPE_THETA = 500000.0
LN_EPS = 1e-5
RMS_EPS = 1e-5

LANES = 128
VMEM_LIMIT_BYTES = 60 * 1024 * 1024

BF16 = jnp.bfloat16
F32 = jnp.float32

LOG2E = 1.4426950408889634

_NT = (((1,), (1,)), ((), ()))


def _dot(a, b):
    return jnp.dot(a, b, preferred_element_type=F32)


def _dot_nt(a, b):
    return lax.dot_general(a, b, _NT, preferred_element_type=F32)


def _layernorm_rows(y, g, b):
    mu = jnp.mean(y, axis=-1, keepdims=True)
    yc = y - mu
    var = jnp.mean(yc * yc, axis=-1, keepdims=True)
    return yc * lax.rsqrt(var + LN_EPS) * g + b


def _params(*semantics):
    return pltpu.CompilerParams(dimension_semantics=semantics,
                                vmem_limit_bytes=VMEM_LIMIT_BYTES)


def _ffn_ln_kernel(x_ref, wg_ref, wu_ref, wd_ref, g_ref, b_ref, o_ref, ob_ref, xb_ref, *, alpha):
    j = pl.program_id(1)

    @pl.when(j == 0)
    def _():
        xb_ref[...] = x_ref[...].astype(BF16)
        o_ref[...] = jnp.zeros_like(o_ref)

    xb = xb_ref[...]
    gate = _dot(xb, wg_ref[...])
    up = _dot(xb, wu_ref[...])
    act = (gate * jax.nn.sigmoid(gate) * up).astype(BF16)
    o_ref[...] += _dot(act, wd_ref[...])

    @pl.when(j == pl.num_programs(1) - 1)
    def _():
        y = alpha * x_ref[...] + 0.5 * o_ref[...]
        out = _layernorm_rows(y, g_ref[...], b_ref[...])
        o_ref[...] = out
        ob_ref[...] = out.astype(BF16)


def _column_chunks(w, tf):
    d, f = w.shape
    return w.astype(BF16).reshape(d, f // tf, tf).transpose(1, 0, 2)


def _ffn_ln(h, wg, wu, wd, g, b, *, alpha, tm, tf):
    n, d = h.shape
    f = wd.shape[0]
    wg, wu = _column_chunks(wg, tf), _column_chunks(wu, tf)
    return pl.pallas_call(
        functools.partial(_ffn_ln_kernel, alpha=alpha),
        grid=(n // tm, f // tf),
        in_specs=[
            pl.BlockSpec((tm, d), lambda i, j: (i, 0)),
            pl.BlockSpec((None, d, tf), lambda i, j: (j, 0, 0)),
            pl.BlockSpec((None, d, tf), lambda i, j: (j, 0, 0)),
            pl.BlockSpec((tf, d), lambda i, j: (j, 0)),
            pl.BlockSpec((1, d), lambda i, j: (0, 0)),
            pl.BlockSpec((1, d), lambda i, j: (0, 0)),
        ],
        out_specs=[pl.BlockSpec((tm, d), lambda i, j: (i, 0)),
                   pl.BlockSpec((tm, d), lambda i, j: (i, 0))],
        out_shape=[jax.ShapeDtypeStruct((n, d), F32), jax.ShapeDtypeStruct((n, d), BF16)],
        scratch_shapes=[pltpu.VMEM((tm, d), BF16)],
        compiler_params=_params("parallel", "arbitrary"),
        name="ffn_ln",
    )(h, wg, wu, wd, g, b)


def _qkv_kernel(h_ref, pos_ref, freq_ref, w_ref, o_ref, c_ref, sa_ref, sb_ref,
                *, sb_scale, diff_scale):
    j = pl.program_id(1)
    tn = o_ref.shape[1]
    half = ROT_DIM // 2

    def rope(scale):
        t = _dot(h_ref[...], w_ref[...])
        c, sa, sb = c_ref[...] * scale, sa_ref[...] * scale, sb_ref[...] * scale
        for hh in range(tn // LANES):
            th = t[:, hh * LANES:(hh + 1) * LANES]
            t_up = pltpu.roll(th, LANES - half, axis=1)
            t_dn = pltpu.roll(th, half, axis=1)
            o_ref[:, hh * LANES:(hh + 1) * LANES] = (th * c + t_up * sa + t_dn * sb).astype(o_ref.dtype)

    @pl.when((j != 3) & (j != 4))
    def _():
        scale = jnp.where(j == 0, sb_scale, 1.0).astype(F32)
        o_ref[...] = (_dot(h_ref[...], w_ref[...]) * scale).astype(o_ref.dtype)

    @pl.when(j == 3)
    def _():
        lane = lax.broadcasted_iota(jnp.int32, (1, LANES), 1) % DIFF_QK_DIM
        ang = pos_ref[...].astype(F32) * freq_ref[...]
        cos = jnp.cos(ang)
        sin = jnp.sin(ang)
        c_ref[...] = jnp.where(lane < ROT_DIM, cos, 1.0)
        sa_ref[...] = jnp.where(lane < half, -sin, 0.0)
        sb_ref[...] = jnp.where((lane >= half) & (lane < ROT_DIM), sin, 0.0)
        rope(diff_scale)

    @pl.when(j == 4)
    def _():
        rope(1.0)


def _qkv_proj(h, pos, freq, w_in, *, tm):
    n, d = h.shape
    width = w_in.shape[1]
    tn = width // 6
    return pl.pallas_call(
        functools.partial(_qkv_kernel, sb_scale=HEAD_DIM ** -0.5 * LOG2E,
                          diff_scale=DIFF_QK_DIM ** -0.5 * LOG2E),
        grid=(n // tm, 6),
        in_specs=[
            pl.BlockSpec((tm, d), lambda i, j: (i, 0)),
            pl.BlockSpec((tm, 1), lambda i, j: (i, 0)),
            pl.BlockSpec((1, LANES), lambda i, j: (0, 0)),
            pl.BlockSpec((d, tn), lambda i, j: (0, j)),
        ],
        out_specs=pl.BlockSpec((tm, tn), lambda i, j: (i, j)),
        out_shape=jax.ShapeDtypeStruct((n, width), BF16),
        scratch_shapes=[pltpu.VMEM((tm, LANES), F32),
                        pltpu.VMEM((tm, LANES), F32),
                        pltpu.VMEM((tm, LANES), F32)],
        compiler_params=_params("parallel", "arbitrary"),
        name="qkv_proj",
    )(h, pos, freq, w_in)


def _neg_abs16(x):
    bits = lax.bitcast_convert_type(x, jnp.uint16) | jnp.uint16(0x8000)
    return lax.bitcast_convert_type(bits, BF16)


def _sb_attn_kernel(q_ref, k_ref, v_ref, o_ref, acc_ref, carry_ref, *, tq, tk):
    i = pl.program_id(2)
    ratio = tq // tk
    row = lax.broadcasted_iota(jnp.int32, (tk, tk), 0)
    col = lax.broadcasted_iota(jnp.int32, (tk, tk), 1)
    tri = jnp.where(row > col, 1.0, 0.0).astype(BF16)

    def scores(c, r0, masked):
        k = k_ref[0, pl.ds(pl.multiple_of(c * tk, tk), tk), :]
        zb = _dot_nt(q_ref[0, r0:, :], k).astype(BF16)
        softplus = jnp.log(1.0 + jnp.exp2(_neg_abs16(zb))) * LOG2E
        log_beta = jnp.minimum(zb, 0.0) - softplus
        log_om = log_beta - zb
        causal = None
        if masked:
            rows = tq - r0
            causal = (lax.broadcasted_iota(jnp.int32, (rows, tk), 1)
                      < lax.broadcasted_iota(jnp.int32, (rows, tk), 0))
            log_om = jnp.where(causal, log_om, jnp.zeros_like(log_om))
        within = _dot(log_om, tri)
        return log_beta, within, within[:, :1] + log_om[:, :1].astype(F32), causal

    def accumulate(c, r0, log_beta, within, total, causal):
        v = v_ref[0, pl.ds(pl.multiple_of(c * tk, tk), tk), :]
        w = jnp.exp2(log_beta.astype(F32) + within + jnp.tile(carry_ref[r0:, :], (1, tk // LANES)))
        if causal is not None:
            w = jnp.where(causal, w, 0.0)
        acc_ref[r0:, :] += _dot(w.astype(BF16), v)
        carry_ref[r0:, :] += total

    def run(blocks):
        staged = [scores(c, r0, masked) for c, r0, masked in blocks]
        for (c, r0, _), st in zip(blocks, staged):
            accumulate(c, r0, *st)

    acc_ref[...] = jnp.zeros_like(acc_ref)
    carry_ref[...] = jnp.zeros_like(carry_ref)
    run([(i * ratio + j, j * tk, True) for j in reversed(range(ratio))])

    @pl.loop(0, i)
    def _(t):
        run([((i - 1 - t) * ratio + j, 0, False) for j in reversed(range(ratio))])

    o_ref[0] = acc_ref[...].astype(o_ref.dtype)


def _sb_attention(qkv, *, n_heads, tq, tk):
    b, s, _ = qkv.shape
    return pl.pallas_call(
        functools.partial(_sb_attn_kernel, tq=tq, tk=tk),
        grid=(b, n_heads, s // tq),
        in_specs=[
            pl.BlockSpec((1, tq, HEAD_DIM), lambda bi, h, i: (bi, i, h)),
            pl.BlockSpec((1, s, HEAD_DIM), lambda bi, h, i: (bi, 0, n_heads + h)),
            pl.BlockSpec((1, s, HEAD_DIM), lambda bi, h, i: (bi, 0, 2 * n_heads + h)),
        ],
        out_specs=pl.BlockSpec((1, tq, HEAD_DIM), lambda bi, h, i: (bi, i, h)),
        out_shape=jax.ShapeDtypeStruct((b, s, n_heads * HEAD_DIM), BF16),
        scratch_shapes=[pltpu.VMEM((tq, HEAD_DIM), F32), pltpu.VMEM((tq, LANES), F32)],
        compiler_params=_params("parallel", "parallel", "arbitrary"),
        name="sb_attn",
    )(qkv, qkv, qkv)


def _diff_attn_kernel(q_ref, k_ref, v_ref, lq1_ref, lk1_ref, lq2_ref, lk2_ref, g_ref, o_ref,
                      qs_ref, m_ref, acc_ref, *, tq, tk, lambda_init):
    i = pl.program_id(2)
    ratio = tq // tk
    q = q_ref[0]
    lane = lax.broadcasted_iota(jnp.int32, (tq, HEAD_DIM), 1)
    zero = jnp.zeros_like(q)
    qs_ref[0] = jnp.where(lane < DIFF_QK_DIM, q, zero)
    qs_ref[1] = jnp.where(lane >= DIFF_QK_DIM, q, zero)
    m_ref[...] = jnp.full_like(m_ref, -jnp.inf)
    acc_ref[...] = jnp.zeros_like(acc_ref)

    def block(c, r0, masked):
        start = pl.multiple_of(c * tk, tk)
        k = k_ref[0, pl.ds(start, tk), :]
        v_ones = jnp.concatenate([v_ref[0, pl.ds(start, tk), :], jnp.ones((tk, LANES), BF16)], axis=1)
        for comp in range(2):
            sc = _dot_nt(qs_ref[comp, r0:, :], k)
            if masked:
                rows = tq - r0
                causal = (lax.broadcasted_iota(jnp.int32, (rows, tk), 1)
                          <= lax.broadcasted_iota(jnp.int32, (rows, tk), 0))
                sc = jnp.where(causal, sc, -jnp.inf)
            m_old = m_ref[comp, r0:, :]
            m_new = jnp.maximum(m_old, jnp.max(sc, axis=-1, keepdims=True))
            scale = jnp.exp2(m_old - m_new)
            pr = jnp.exp2(sc - jnp.tile(m_new, (1, tk // LANES)))
            m_ref[comp, r0:, :] = m_new
            acc_ref[comp, r0:, :] = (jnp.tile(scale, (1, 2)) * acc_ref[comp, r0:, :]
                                     + _dot(pr.astype(BF16), v_ones))

    for j in reversed(range(ratio)):
        block(i * ratio + j, j * tk, True)

    @pl.loop(0, i)
    def _(t):
        for j in reversed(range(ratio)):
            block((i - 1 - t) * ratio + j, 0, False)

    lam = (jnp.exp(jnp.sum(lq1_ref[...] * lk1_ref[...], axis=-1, keepdims=True))
           - jnp.exp(jnp.sum(lq2_ref[...] * lk2_ref[...], axis=-1, keepdims=True))
           + lambda_init)
    o = (acc_ref[0, :, :HEAD_DIM] / acc_ref[0, :, HEAD_DIM:]
         - lam * (acc_ref[1, :, :HEAD_DIM] / acc_ref[1, :, HEAD_DIM:]))
    o = o * lax.rsqrt(jnp.mean(o * o, axis=-1, keepdims=True) + RMS_EPS)
    o_ref[0] = (o * g_ref[...] * (1.0 - lambda_init)).astype(o_ref.dtype)


def _diff_attention(qkv, lq1, lk1, lq2, lk2, subln_g, *, n_heads, col0, lambda_init, tq, tk):
    b, s, _ = qkv.shape
    small = lambda width: pl.BlockSpec((1, width), lambda bi, h, i: (0, 0))
    return pl.pallas_call(
        functools.partial(_diff_attn_kernel, tq=tq, tk=tk, lambda_init=lambda_init),
        grid=(b, n_heads, s // tq),
        in_specs=[
            pl.BlockSpec((1, tq, HEAD_DIM), lambda bi, h, i: (bi, i, col0 + h)),
            pl.BlockSpec((1, s, HEAD_DIM), lambda bi, h, i: (bi, 0, col0 + n_heads + h)),
            pl.BlockSpec((1, s, HEAD_DIM), lambda bi, h, i: (bi, 0, col0 + 2 * n_heads + h)),
            small(DIFF_QK_DIM), small(DIFF_QK_DIM), small(DIFF_QK_DIM), small(DIFF_QK_DIM),
            small(HEAD_DIM),
        ],
        out_specs=pl.BlockSpec((1, tq, HEAD_DIM), lambda bi, h, i: (bi, i, h)),
        out_shape=jax.ShapeDtypeStruct((b, s, n_heads * HEAD_DIM), BF16),
        scratch_shapes=[pltpu.VMEM((2, tq, HEAD_DIM), BF16), pltpu.VMEM((2, tq, LANES), F32),
                        pltpu.VMEM((2, tq, HEAD_DIM + LANES), F32)],
        compiler_params=_params("parallel", "parallel", "arbitrary"),
        name="diff_attn",
    )(qkv, qkv, qkv, lq1, lk1, lq2, lk2, subln_g)


def _outproj_ln_kernel(h_ref, osb_ref, odf_ref, wa_ref, wb_ref, g_ref, b_ref, o_ref, *, alpha):
    mix = _dot(osb_ref[...], wa_ref[...]) + _dot(odf_ref[...], wb_ref[...])
    o_ref[...] = _layernorm_rows(alpha * h_ref[...] + mix, g_ref[...], b_ref[...])


def _outproj_ln(h, o_sb, o_df, w_out, g, b, *, alpha, tm):
    n, d = h.shape
    wa_rows, wb_rows = o_sb.shape[1], o_df.shape[1]
    assert wa_rows == wb_rows and wa_rows + wb_rows == w_out.shape[0]
    return pl.pallas_call(
        functools.partial(_outproj_ln_kernel, alpha=alpha),
        grid=(n // tm,),
        in_specs=[
            pl.BlockSpec((tm, d), lambda i: (i, 0)),
            pl.BlockSpec((tm, wa_rows), lambda i: (i, 0)),
            pl.BlockSpec((tm, wb_rows), lambda i: (i, 0)),
            pl.BlockSpec((wa_rows, d), lambda i: (0, 0)),
            pl.BlockSpec((wb_rows, d), lambda i: (1, 0)),
            pl.BlockSpec((1, d), lambda i: (0, 0)),
            pl.BlockSpec((1, d), lambda i: (0, 0)),
        ],
        out_specs=pl.BlockSpec((tm, d), lambda i: (i, 0)),
        out_shape=jax.ShapeDtypeStruct((n, d), F32),
        compiler_params=_params("parallel"),
        name="outproj_ln",
    )(h, o_sb, o_df, w_out, w_out, g, b)


def _ple_ln_kernel(h_ref, hb_ref, p_ref, wg_ref, bg_ref, wp_ref, g_ref, b_ref, o_ref, *, alpha):
    gate = jax.nn.sigmoid(_dot(hb_ref[...], wg_ref[...]) + bg_ref[...])
    emb = _dot(p_ref[...].astype(BF16), wp_ref[...])
    o_ref[...] = _layernorm_rows(alpha * h_ref[...] + gate * emb, g_ref[...], b_ref[...])


def _ple_ln(h, hb, p, wg, bg, wp, g, b, *, alpha, tm):
    n, d = h.shape
    dp = p.shape[1]
    return pl.pallas_call(
        functools.partial(_ple_ln_kernel, alpha=alpha),
        grid=(n // tm,),
        in_specs=[
            pl.BlockSpec((tm, d), lambda i: (i, 0)),
            pl.BlockSpec((tm, d), lambda i: (i, 0)),
            pl.BlockSpec((tm, dp), lambda i: (i, 0)),
            pl.BlockSpec((d, d), lambda i: (0, 0)),
            pl.BlockSpec((1, d), lambda i: (0, 0)),
            pl.BlockSpec((dp, d), lambda i: (0, 0)),
            pl.BlockSpec((1, d), lambda i: (0, 0)),
            pl.BlockSpec((1, d), lambda i: (0, 0)),
        ],
        out_specs=pl.BlockSpec((tm, d), lambda i: (i, 0)),
        out_shape=jax.ShapeDtypeStruct((n, d), F32),
        compiler_params=_params("parallel"),
        name="ple_ln",
    )(h, hb, p, wg, bg, wp, g, b)


def _tile(n, target):
    t = min(n, target)
    assert n % t == 0, (n, t)
    return t


def kernel(x, p, positions, ln_g, ln_b, ffn1_w_gate, ffn1_w_up, ffn1_w_down, w_in, w_out,
           lambda_q1, lambda_k1, lambda_q2, lambda_k2, diff_subln_g,
           ffn2_w_gate, ffn2_w_up, ffn2_w_down, ple_w_gate, ple_b_gate, ple_w_proj):
    batch, seq, d = x.shape
    depth = ln_g.shape[0]
    n = batch * seq
    n_sb = d // (2 * HEAD_DIM)
    n_diff = d // (2 * HEAD_DIM)
    alpha = (2.0 * depth) ** 0.25

    tm_ffn = _tile(n, 1024)
    tf = _tile(ffn1_w_gate.shape[2], 256)
    tm_proj = _tile(n, 1024)
    tm_row = _tile(n, 512)
    tq = _tile(seq, 2048)
    tk = _tile(tq, 256)

    inv_freq = ROPE_THETA ** (-jnp.arange(0, ROT_DIM, 2, dtype=F32) / ROT_DIM)
    lane = jnp.arange(LANES) % DIFF_QK_DIM
    freq = jnp.where(lane < ROT_DIM, inv_freq[lane % (ROT_DIM // 2)], 0.0).reshape(1, LANES)
    pos = positions.reshape(n, 1)

    row = lambda v: v.reshape(1, -1)
    h = x.reshape(n, d)
    for i in range(depth):
        lambda_init = 0.8 - 0.6 * math.exp(-0.3 * i)
        h, hb = _ffn_ln(h, ffn1_w_gate[i], ffn1_w_up[i], ffn1_w_down[i].astype(BF16), row(ln_g[i, 0]), row(ln_b[i, 0]),
                        alpha=alpha, tm=tm_ffn, tf=tf)
        qkv = _qkv_proj(hb, pos, freq, w_in[i].astype(BF16), tm=tm_proj)
        qkv = qkv.reshape(batch, seq, -1)
        o_sb = _sb_attention(qkv, n_heads=n_sb, tq=tq, tk=tk)
        o_df = _diff_attention(qkv, row(lambda_q1[i]), row(lambda_k1[i]), row(lambda_q2[i]),
                               row(lambda_k2[i]), row(diff_subln_g[i]), n_heads=n_diff,
                               col0=3 * n_sb, lambda_init=lambda_init, tq=tq, tk=tk)
        h = _outproj_ln(h, o_sb.reshape(n, -1), o_df.reshape(n, -1), w_out[i].astype(BF16),
                        row(ln_g[i, 1]), row(ln_b[i, 1]), alpha=alpha, tm=tm_row)
        h, hb = _ffn_ln(h, ffn2_w_gate[i], ffn2_w_up[i], ffn2_w_down[i].astype(BF16), row(ln_g[i, 2]), row(ln_b[i, 2]),
                        alpha=alpha, tm=tm_ffn, tf=tf)
        h = _ple_ln(h, hb, p[i].reshape(n, -1), ple_w_gate[i].astype(BF16), row(ple_b_gate[i]),
                    ple_w_proj[i].astype(BF16), row(ln_g[i, 3]), row(ln_b[i, 3]),
                    alpha=alpha, tm=tm_row)
    return h.reshape(batch, seq, d)
```

```python
import functools
import math

import jax
import jax.numpy as jnp
from jax import lax
from jax.experimental import pallas as pl
from jax.experimental.pallas import tpu as pltpu

HEAD_DIM = 128
DIFF_QK_DIM = HEAD_DIM // 2
ROT_DIM = DIFF_QK_DIM // 4
ROPE_THETA = 500000.0
LN_EPS = 1e-5
RMS_EPS = 1e-5

LANES = 128
VMEM_LIMIT_BYTES = 60 * 1024 * 1024

BF16 = jnp.bfloat16
F32 = jnp.float32

LOG2E = 1.4426950408889634
ROW_GROUPS = 2
FFN_EDGE_GROUPS = 2

_NT = (((1,), (1,)), ((), ()))


def _dot(a, b):
    return jnp.dot(a, b, preferred_element_type=F32)


def _dot_nt(a, b):
    return lax.dot_general(a, b, _NT, preferred_element_type=F32)


def _layernorm_rows(y, g, b):
    mu = jnp.mean(y, axis=-1, keepdims=True)
    yc = y - mu
    var = jnp.mean(yc * yc, axis=-1, keepdims=True)
    return yc * lax.rsqrt(var + LN_EPS) * g + b


def _params(*semantics):
    return pltpu.CompilerParams(dimension_semantics=semantics,
                                vmem_limit_bytes=VMEM_LIMIT_BYTES)


def _ffn_ln_kernel(x_ref, wg_ref, wu_ref, wd_ref, g_ref, b_ref, o_ref, ob_ref, xb_ref, *, alpha):
    j = pl.program_id(1)
    last = pl.num_programs(1) - 1
    tm = o_ref.shape[0]

    def chunk(rows, first, final):
        if first:
            xb = x_ref[rows, :].astype(BF16)
            xb_ref[rows, :] = xb
        else:
            xb = xb_ref[rows, :]
        gate = _dot(xb, wg_ref[...])
        up = _dot(xb, wu_ref[...])
        act = (gate * jax.nn.sigmoid(gate) * up).astype(BF16)
        acc = _dot(act, wd_ref[...])
        if not first:
            acc = o_ref[rows, :] + acc
        if final:
            out = _layernorm_rows(alpha * x_ref[rows, :] + 0.5 * acc, g_ref[...], b_ref[...])
            o_ref[rows, :] = out
            ob_ref[rows, :] = out.astype(BF16)
        else:
            o_ref[rows, :] = acc

    groups = [slice(r, r + tm // FFN_EDGE_GROUPS) for r in range(0, tm, tm // FFN_EDGE_GROUPS)]

    @pl.when(j == 0)
    def _():
        for rows in groups:
            chunk(rows, True, False)

    @pl.when((j > 0) & (j < last))
    def _():
        chunk(slice(0, tm), False, False)

    @pl.when(j == last)
    def _():
        for rows in groups:
            chunk(rows, False, True)


def _ffn_ln(h, wg, wu, wd, g, b, *, alpha, tm, tf):
    n, d = h.shape
    f = wg.shape[1]
    assert f // tf >= 2, "the first and the final d_ff step are distinct code paths"
    return pl.pallas_call(
        functools.partial(_ffn_ln_kernel, alpha=alpha),
        grid=(n // tm, f // tf),
        in_specs=[
            pl.BlockSpec((tm, d), lambda i, j: (i, 0)),
            pl.BlockSpec((d, tf), lambda i, j: (0, j)),
            pl.BlockSpec((d, tf), lambda i, j: (0, j)),
            pl.BlockSpec((tf, d), lambda i, j: (j, 0)),
            pl.BlockSpec((1, d), lambda i, j: (0, 0)),
            pl.BlockSpec((1, d), lambda i, j: (0, 0)),
        ],
        out_specs=[pl.BlockSpec((tm, d), lambda i, j: (i, 0)),
                   pl.BlockSpec((tm, d), lambda i, j: (i, 0))],
        out_shape=[jax.ShapeDtypeStruct((n, d), F32), jax.ShapeDtypeStruct((n, d), BF16)],
        scratch_shapes=[pltpu.VMEM((tm, d), BF16)],
        compiler_params=_params("parallel", "arbitrary"),
        name="ffn_ln",
    )(h, wg, wu, wd, g, b)


def _qkv_kernel(h_ref, pos_ref, freq_ref, w_ref, o_ref, c_ref, sa_ref, sb_ref,
                *, sb_scale, diff_scale):
    j = pl.program_id(1)
    tn = o_ref.shape[1]
    half = ROT_DIM // 2

    def rope(scale):
        t = _dot(h_ref[...], w_ref[...])
        c, sa, sb = c_ref[...] * scale, sa_ref[...] * scale, sb_ref[...] * scale
        for hh in range(tn // LANES):
            th = t[:, hh * LANES:(hh + 1) * LANES]
            t_up = pltpu.roll(th, LANES - half, axis=1)
            t_dn = pltpu.roll(th, half, axis=1)
            o_ref[:, hh * LANES:(hh + 1) * LANES] = (th * c + t_up * sa + t_dn * sb).astype(o_ref.dtype)

    @pl.when((j != 3) & (j != 4))
    def _():
        scale = jnp.where(j == 0, sb_scale, 1.0).astype(F32)
        o_ref[...] = (_dot(h_ref[...], w_ref[...]) * scale).astype(o_ref.dtype)

    @pl.when(j == 3)
    def _():
        lane = lax.broadcasted_iota(jnp.int32, (1, LANES), 1) % DIFF_QK_DIM
        ang = pos_ref[...].astype(F32) * freq_ref[...]
        cos = jnp.cos(ang)
        sin = jnp.sin(ang)
        c_ref[...] = jnp.where(lane < ROT_DIM, cos, 1.0)
        sa_ref[...] = jnp.where(lane < half, -sin, 0.0)
        sb_ref[...] = jnp.where((lane >= half) & (lane < ROT_DIM), sin, 0.0)
        rope(diff_scale)

    @pl.when(j == 4)
    def _():
        rope(1.0)


def _qkv_proj(h, pos, freq, w_in, *, tm):
    n, d = h.shape
    width = w_in.shape[1]
    tn = width // 6
    return pl.pallas_call(
        functools.partial(_qkv_kernel, sb_scale=HEAD_DIM ** -0.5 * LOG2E,
                          diff_scale=DIFF_QK_DIM ** -0.5 * LOG2E),
        grid=(n // tm, 6),
        in_specs=[
            pl.BlockSpec((tm, d), lambda i, j: (i, 0)),
            pl.BlockSpec((tm, 1), lambda i, j: (i, 0)),
            pl.BlockSpec((1, LANES), lambda i, j: (0, 0)),
            pl.BlockSpec((d, tn), lambda i, j: (0, j)),
        ],
        out_specs=pl.BlockSpec((tm, tn), lambda i, j: (i, j)),
        out_shape=jax.ShapeDtypeStruct((n, width), BF16),
        scratch_shapes=[pltpu.VMEM((tm, LANES), F32),
                        pltpu.VMEM((tm, LANES), F32),
                        pltpu.VMEM((tm, LANES), F32)],
        compiler_params=_params("parallel", "arbitrary"),
        name="qkv_proj",
    )(h, pos, freq, w_in)


def _neg_abs16(x):
    bits = lax.bitcast_convert_type(x, jnp.uint16) | jnp.uint16(0x8000)
    return lax.bitcast_convert_type(bits, BF16)


def _sb_attn_kernel(q_ref, k_ref, v_ref, o_ref, acc_ref, carry_ref, *, tq, tk):
    i = pl.program_id(2)
    ratio = tq // tk
    row = lax.broadcasted_iota(jnp.int32, (tk, tk), 0)
    col = lax.broadcasted_iota(jnp.int32, (tk, tk), 1)
    tri = jnp.where(row > col, 1.0, 0.0).astype(BF16)

    def scores(c, r0, masked):
        k = k_ref[0, pl.ds(pl.multiple_of(c * tk, tk), tk), :]
        zb = _dot_nt(q_ref[0, r0:, :], k).astype(BF16)
        softplus = jnp.log(1.0 + jnp.exp2(_neg_abs16(zb))) * LOG2E
        log_beta = jnp.minimum(zb, 0.0) - softplus
        log_om = log_beta - zb
        causal = None
        if masked:
            rows = tq - r0
            causal = (lax.broadcasted_iota(jnp.int32, (rows, tk), 1)
                      < lax.broadcasted_iota(jnp.int32, (rows, tk), 0))
            log_om = jnp.where(causal, log_om, jnp.zeros_like(log_om))
        within = _dot(log_om, tri)
        return log_beta, within, within[:, :1] + log_om[:, :1].astype(F32), causal

    def accumulate(c, r0, log_beta, within, total, causal):
        v = v_ref[0, pl.ds(pl.multiple_of(c * tk, tk), tk), :]
        w = jnp.exp2(log_beta.astype(F32) + within + jnp.tile(carry_ref[r0:, :], (1, tk // LANES)))
        if causal is not None:
            w = jnp.where(causal, w, 0.0)
        acc_ref[r0:, :] += _dot(w.astype(BF16), v)
        carry_ref[r0:, :] += total

    def run(blocks):
        staged = [scores(c, r0, masked) for c, r0, masked in blocks]
        for (c, r0, _), st in zip(blocks, staged):
            accumulate(c, r0, *st)

    acc_ref[...] = jnp.zeros_like(acc_ref)
    carry_ref[...] = jnp.zeros_like(carry_ref)
    run([(i * ratio + j, j * tk, True) for j in reversed(range(ratio))])

    @pl.loop(0, i)
    def _(t):
        run([((i - 1 - t) * ratio + j, 0, False) for j in reversed(range(ratio))])

    o_ref[0] = acc_ref[...].astype(o_ref.dtype)


def _sb_attention(qkv, *, n_heads, tq, tk):
    b, s, _ = qkv.shape
    return pl.pallas_call(
        functools.partial(_sb_attn_kernel, tq=tq, tk=tk),
        grid=(b, n_heads, s // tq),
        in_specs=[
            pl.BlockSpec((1, tq, HEAD_DIM), lambda bi, h, i: (bi, i, h)),
            pl.BlockSpec((1, s, HEAD_DIM), lambda bi, h, i: (bi, 0, n_heads + h)),
            pl.BlockSpec((1, s, HEAD_DIM), lambda bi, h, i: (bi, 0, 2 * n_heads + h)),
        ],
        out_specs=pl.BlockSpec((1, tq, HEAD_DIM), lambda bi, h, i: (bi, i, h)),
        out_shape=jax.ShapeDtypeStruct((b, s, n_heads * HEAD_DIM), BF16),
        scratch_shapes=[pltpu.VMEM((tq, HEAD_DIM), F32), pltpu.VMEM((tq, LANES), F32)],
        compiler_params=_params("parallel", "parallel", "arbitrary"),
        name="sb_attn",
    )(qkv, qkv, qkv)


def _diff_attn_kernel(q_ref, k_ref, v_ref, lq1_ref, lk1_ref, lq2_ref, lk2_ref, g_ref, o_ref,
                      qs_ref, m_ref, acc_ref, *, tq, tk, lambda_init):
    i = pl.program_id(2)
    ratio = tq // tk
    q = q_ref[0]
    lane = lax.broadcasted_iota(jnp.int32, (tq, HEAD_DIM), 1)
    zero = jnp.zeros_like(q)
    qs_ref[0] = jnp.where(lane < DIFF_QK_DIM, q, zero)
    qs_ref[1] = jnp.where(lane >= DIFF_QK_DIM, q, zero)
    m_ref[...] = jnp.full_like(m_ref, -jnp.inf)
    acc_ref[...] = jnp.zeros_like(acc_ref)

    def block(c, r0, masked):
        start = pl.multiple_of(c * tk, tk)
        k = k_ref[0, pl.ds(start, tk), :]
        v_ones = jnp.concatenate([v_ref[0, pl.ds(start, tk), :], jnp.ones((tk, LANES), BF16)], axis=1)
        for comp in range(2):
            sc = _dot_nt(qs_ref[comp, r0:, :], k)
            if masked:
                rows = tq - r0
                causal = (lax.broadcasted_iota(jnp.int32, (rows, tk), 1)
                          <= lax.broadcasted_iota(jnp.int32, (rows, tk), 0))
                sc = jnp.where(causal, sc, -jnp.inf)
            m_old = m_ref[comp, r0:, :]
            m_new = jnp.maximum(m_old, jnp.max(sc, axis=-1, keepdims=True))
            scale = jnp.exp2(m_old - m_new)
            pr = jnp.exp2(sc - jnp.tile(m_new, (1, tk // LANES)))
            m_ref[comp, r0:, :] = m_new
            acc_ref[comp, r0:, :] = (jnp.tile(scale, (1, 2)) * acc_ref[comp, r0:, :]
                                     + _dot(pr.astype(BF16), v_ones))

    for j in reversed(range(ratio)):
        block(i * ratio + j, j * tk, True)

    @pl.loop(0, i)
    def _(t):
        for j in reversed(range(ratio)):
            block((i - 1 - t) * ratio + j, 0, False)

    lam = (jnp.exp(jnp.sum(lq1_ref[...] * lk1_ref[...], axis=-1, keepdims=True))
           - jnp.exp(jnp.sum(lq2_ref[...] * lk2_ref[...], axis=-1, keepdims=True))
           + lambda_init)
    o = (acc_ref[0, :, :HEAD_DIM] / acc_ref[0, :, HEAD_DIM:]
         - lam * (acc_ref[1, :, :HEAD_DIM] / acc_ref[1, :, HEAD_DIM:]))
    o = o * lax.rsqrt(jnp.mean(o * o, axis=-1, keepdims=True) + RMS_EPS)
    o_ref[0] = (o * g_ref[...] * (1.0 - lambda_init)).astype(o_ref.dtype)


def _diff_attention(qkv, lq1, lk1, lq2, lk2, subln_g, *, n_heads, col0, lambda_init, tq, tk):
    b, s, _ = qkv.shape
    small = lambda width: pl.BlockSpec((1, width), lambda bi, h, i: (0, 0))
    return pl.pallas_call(
        functools.partial(_diff_attn_kernel, tq=tq, tk=tk, lambda_init=lambda_init),
        grid=(b, n_heads, s // tq),
        in_specs=[
            pl.BlockSpec((1, tq, HEAD_DIM), lambda bi, h, i: (bi, i, col0 + h)),
            pl.BlockSpec((1, s, HEAD_DIM), lambda bi, h, i: (bi, 0, col0 + n_heads + h)),
            pl.BlockSpec((1, s, HEAD_DIM), lambda bi, h, i: (bi, 0, col0 + 2 * n_heads + h)),
            small(DIFF_QK_DIM), small(DIFF_QK_DIM), small(DIFF_QK_DIM), small(DIFF_QK_DIM),
            small(HEAD_DIM),
        ],
        out_specs=pl.BlockSpec((1, tq, HEAD_DIM), lambda bi, h, i: (bi, i, h)),
        out_shape=jax.ShapeDtypeStruct((b, s, n_heads * HEAD_DIM), BF16),
        scratch_shapes=[pltpu.VMEM((2, tq, HEAD_DIM), BF16), pltpu.VMEM((2, tq, LANES), F32),
                        pltpu.VMEM((2, tq, HEAD_DIM + LANES), F32)],
        compiler_params=_params("parallel", "parallel", "arbitrary"),
        name="diff_attn",
    )(qkv, qkv, qkv, lq1, lk1, lq2, lk2, subln_g)


def _outproj_ln_kernel(h_ref, osb_ref, odf_ref, wa_ref, wb_ref, g_ref, b_ref, o_ref, *, alpha):
    tm = o_ref.shape[0]
    for r in range(0, tm, tm // ROW_GROUPS):
        rows = slice(r, r + tm // ROW_GROUPS)
        mix = _dot(osb_ref[rows, :], wa_ref[...]) + _dot(odf_ref[rows, :], wb_ref[...])
        o_ref[rows, :] = _layernorm_rows(alpha * h_ref[rows, :] + mix, g_ref[...], b_ref[...])


def _outproj_ln(h, o_sb, o_df, w_a, w_b, g, b, *, alpha, tm):
    n, d = h.shape
    wa_rows, wb_rows = w_a.shape[0], w_b.shape[0]
    return pl.pallas_call(
        functools.partial(_outproj_ln_kernel, alpha=alpha),
        grid=(n // tm,),
        in_specs=[
            pl.BlockSpec((tm, d), lambda i: (i, 0)),
            pl.BlockSpec((tm, wa_rows), lambda i: (i, 0)),
            pl.BlockSpec((tm, wb_rows), lambda i: (i, 0)),
            pl.BlockSpec((wa_rows, d), lambda i: (0, 0)),
            pl.BlockSpec((wb_rows, d), lambda i: (0, 0)),
            pl.BlockSpec((1, d), lambda i: (0, 0)),
            pl.BlockSpec((1, d), lambda i: (0, 0)),
        ],
        out_specs=pl.BlockSpec((tm, d), lambda i: (i, 0)),
        out_shape=jax.ShapeDtypeStruct((n, d), F32),
        compiler_params=_params("parallel"),
        name="outproj_ln",
    )(h, o_sb, o_df, w_a, w_b, g, b)


def _ple_ln_kernel(h_ref, hb_ref, p_ref, wg_ref, bg_ref, wp_ref, g_ref, b_ref, o_ref, *, alpha):
    tm = o_ref.shape[0]
    for r in range(0, tm, tm // ROW_GROUPS):
        rows = slice(r, r + tm // ROW_GROUPS)
        gate = jax.nn.sigmoid(_dot(hb_ref[rows, :], wg_ref[...]) + bg_ref[...])
        emb = _dot(p_ref[rows, :].astype(BF16), wp_ref[...])
        o_ref[rows, :] = _layernorm_rows(alpha * h_ref[rows, :] + gate * emb, g_ref[...], b_ref[...])


def _ple_ln(h, hb, p, wg, bg, wp, g, b, *, alpha, tm):
    n, d = h.shape
    dp = p.shape[1]
    return pl.pallas_call(
        functools.partial(_ple_ln_kernel, alpha=alpha),
        grid=(n // tm,),
        in_specs=[
            pl.BlockSpec((tm, d), lambda i: (i, 0)),
            pl.BlockSpec((tm, d), lambda i: (i, 0)),
            pl.BlockSpec((tm, dp), lambda i: (i, 0)),
            pl.BlockSpec((d, d), lambda i: (0, 0)),
            pl.BlockSpec((1, d), lambda i: (0, 0)),
            pl.BlockSpec((dp, d), lambda i: (0, 0)),
            pl.BlockSpec((1, d), lambda i: (0, 0)),
            pl.BlockSpec((1, d), lambda i: (0, 0)),
        ],
        out_specs=pl.BlockSpec((tm, d), lambda i: (i, 0)),
        out_shape=jax.ShapeDtypeStruct((n, d), F32),
        compiler_params=_params("parallel"),
        name="ple_ln",
    )(h, hb, p, wg, bg, wp, g, b)


def _tile(n, target):
    t = min(n, target)
    assert n % t == 0, (n, t)
    return t


def kernel(x, p, positions, ln_g, ln_b, ffn1_w_gate, ffn1_w_up, ffn1_w_down, w_in, w_out,
           lambda_q1, lambda_k1, lambda_q2, lambda_k2, diff_subln_g,
           ffn2_w_gate, ffn2_w_up, ffn2_w_down, ple_w_gate, ple_b_gate, ple_w_proj):
    batch, seq, d = x.shape
    depth = ln_g.shape[0]
    n = batch * seq
    n_sb = d // (2 * HEAD_DIM)
    n_diff = d // (2 * HEAD_DIM)
    sb_width = n_sb * HEAD_DIM
    alpha = (2.0 * depth) ** 0.25

    tm_ffn = _tile(n, 1024)
    tf = _tile(ffn1_w_gate.shape[2], 256)
    tm_proj = _tile(n, 1024)
    tm_row = _tile(n, 512)
    tq = _tile(seq, 2048)
    tk = _tile(tq, 256)

    inv_freq = ROPE_THETA ** (-jnp.arange(0, ROT_DIM, 2, dtype=F32) / ROT_DIM)
    lane = jnp.arange(LANES) % DIFF_QK_DIM
    freq = jnp.where(lane < ROT_DIM, inv_freq[lane % (ROT_DIM // 2)], 0.0).reshape(1, LANES)
    pos = positions.reshape(n, 1)

    row = lambda v: v.reshape(1, -1)
    h = x.reshape(n, d)
    for i in range(depth):
        lambda_init = 0.8 - 0.6 * math.exp(-0.3 * i)
        h, hb = _ffn_ln(h, ffn1_w_gate[i].astype(BF16), ffn1_w_up[i].astype(BF16),
                        ffn1_w_down[i].astype(BF16), row(ln_g[i, 0]), row(ln_b[i, 0]),
                        alpha=alpha, tm=tm_ffn, tf=tf)
        qkv = _qkv_proj(hb, pos, freq, w_in[i].astype(BF16), tm=tm_proj)
        qkv = qkv.reshape(batch, seq, -1)
        o_sb = _sb_attention(qkv, n_heads=n_sb, tq=tq, tk=tk)
        o_df = _diff_attention(qkv, row(lambda_q1[i]), row(lambda_k1[i]), row(lambda_q2[i]),
                               row(lambda_k2[i]), row(diff_subln_g[i]), n_heads=n_diff,
                               col0=3 * n_sb, lambda_init=lambda_init, tq=tq, tk=tk)
        w_o = w_out[i].astype(BF16)
        h = _outproj_ln(h, o_sb.reshape(n, -1), o_df.reshape(n, -1), w_o[:sb_width], w_o[sb_width:],
                        row(ln_g[i, 1]), row(ln_b[i, 1]), alpha=alpha, tm=tm_row)
        h, hb = _ffn_ln(h, ffn2_w_gate[i].astype(BF16), ffn2_w_up[i].astype(BF16),
                        ffn2_w_down[i].astype(BF16), row(ln_g[i, 2]), row(ln_b[i, 2]),
                        alpha=alpha, tm=tm_ffn, tf=tf)
        h = _ple_ln(h, hb, p[i].reshape(n, -1), ple_w_gate[i].astype(BF16), row(ple_b_gate[i]),
                    ple_w_proj[i].astype(BF16), row(ln_g[i, 3]), row(ln_b[i, 3]),
                    alpha=alpha, tm=tm_row)
    return h.reshape(batch, seq, d)
```

```python
import functools
import math

import jax
import jax.numpy as jnp
from jax import lax
from jax.experimental import pallas as pl
from jax.experimental.pallas import tpu as pltpu

HEAD_DIM = 128
DIFF_QK_DIM = HEAD_DIM // 2
ROT_DIM = DIFF_QK_DIM // 4
ROPE_THETA = 500000.0
LN_EPS = 1e-5
RMS_EPS = 1e-5

LANES = 128
VMEM_LIMIT_BYTES = 60 * 1024 * 1024

BF16 = jnp.bfloat16
F32 = jnp.float32

LOG2E = 1.4426950408889634
ROW_GROUPS = 2

_NT = (((1,), (1,)), ((), ()))


def _dot(a, b):
    return jnp.dot(a, b, preferred_element_type=F32)


def _dot_nt(a, b):
    return lax.dot_general(a, b, _NT, preferred_element_type=F32)


def _layernorm_rows(y, g, b):
    mu = jnp.mean(y, axis=-1, keepdims=True)
    yc = y - mu
    var = jnp.mean(yc * yc, axis=-1, keepdims=True)
    return yc * lax.rsqrt(var + LN_EPS) * g + b


def _params(*semantics):
    return pltpu.CompilerParams(dimension_semantics=semantics,
                                vmem_limit_bytes=VMEM_LIMIT_BYTES)


def _ffn_ln_kernel(x_ref, wg_ref, wu_ref, wd_ref, g_ref, b_ref, o_ref, ob_ref, *, alpha, tf):
    xb = x_ref[...].astype(BF16)
    for c in range(wg_ref.shape[1] // tf):
        cols = slice(c * tf, (c + 1) * tf)
        gate = _dot(xb, wg_ref[:, cols])
        up = _dot(xb, wu_ref[:, cols])
        act = (gate * jax.nn.sigmoid(gate) * up).astype(BF16)
        part = _dot(act, wd_ref[cols, :])
        if c == 0:
            o_ref[...] = part
        else:
            o_ref[...] += part
    out = _layernorm_rows(alpha * x_ref[...] + 0.5 * o_ref[...], g_ref[...], b_ref[...])
    o_ref[...] = out
    ob_ref[...] = out.astype(BF16)


def _ffn_ln(h, wg, wu, wd, g, b, *, alpha, tm, tf):
    n, d = h.shape
    f = wg.shape[1]
    resident = lambda shape: pl.BlockSpec(shape, lambda i: (0, 0), pipeline_mode=pl.Buffered(1))
    return pl.pallas_call(
        functools.partial(_ffn_ln_kernel, alpha=alpha, tf=tf),
        grid=(n // tm,),
        in_specs=[
            pl.BlockSpec((tm, d), lambda i: (i, 0)),
            resident((d, f)),
            resident((d, f)),
            resident((f, d)),
            resident((1, d)),
            resident((1, d)),
        ],
        out_specs=[pl.BlockSpec((tm, d), lambda i: (i, 0)),
                   pl.BlockSpec((tm, d), lambda i: (i, 0))],
        out_shape=[jax.ShapeDtypeStruct((n, d), F32), jax.ShapeDtypeStruct((n, d), BF16)],
        compiler_params=_params("parallel"),
        name="ffn_ln",
    )(h, wg, wu, wd, g, b)


def _qkv_kernel(h_ref, pos_ref, freq_ref, w_ref, o_ref, c_ref, sa_ref, sb_ref,
                *, sb_scale, diff_scale):
    j = pl.program_id(1)
    tn = o_ref.shape[1]
    half = ROT_DIM // 2

    def rope(scale):
        t = _dot(h_ref[...], w_ref[...])
        c, sa, sb = c_ref[...] * scale, sa_ref[...] * scale, sb_ref[...] * scale
        for hh in range(tn // LANES):
            th = t[:, hh * LANES:(hh + 1) * LANES]
            t_up = pltpu.roll(th, LANES - half, axis=1)
            t_dn = pltpu.roll(th, half, axis=1)
            o_ref[:, hh * LANES:(hh + 1) * LANES] = (th * c + t_up * sa + t_dn * sb).astype(o_ref.dtype)

    @pl.when((j != 3) & (j != 4))
    def _():
        scale = jnp.where(j == 0, sb_scale, 1.0).astype(F32)
        o_ref[...] = (_dot(h_ref[...], w_ref[...]) * scale).astype(o_ref.dtype)

    @pl.when(j == 3)
    def _():
        lane = lax.broadcasted_iota(jnp.int32, (1, LANES), 1) % DIFF_QK_DIM
        ang = pos_ref[...].astype(F32) * freq_ref[...]
        cos = jnp.cos(ang)
        sin = jnp.sin(ang)
        c_ref[...] = jnp.where(lane < ROT_DIM, cos, 1.0)
        sa_ref[...] = jnp.where(lane < half, -sin, 0.0)
        sb_ref[...] = jnp.where((lane >= half) & (lane < ROT_DIM), sin, 0.0)
        rope(diff_scale)

    @pl.when(j == 4)
    def _():
        rope(1.0)


def _qkv_proj(h, pos, freq, w_in, *, tm):
    n, d = h.shape
    width = w_in.shape[1]
    tn = width // 6
    return pl.pallas_call(
        functools.partial(_qkv_kernel, sb_scale=HEAD_DIM ** -0.5 * LOG2E,
                          diff_scale=DIFF_QK_DIM ** -0.5 * LOG2E),
        grid=(n // tm, 6),
        in_specs=[
            pl.BlockSpec((tm, d), lambda i, j: (i, 0)),
            pl.BlockSpec((tm, 1), lambda i, j: (i, 0)),
            pl.BlockSpec((1, LANES), lambda i, j: (0, 0)),
            pl.BlockSpec((d, tn), lambda i, j: (0, j)),
        ],
        out_specs=pl.BlockSpec((tm, tn), lambda i, j: (i, j)),
        out_shape=jax.ShapeDtypeStruct((n, width), BF16),
        scratch_shapes=[pltpu.VMEM((tm, LANES), F32),
                        pltpu.VMEM((tm, LANES), F32),
                        pltpu.VMEM((tm, LANES), F32)],
        compiler_params=_params("parallel", "arbitrary"),
        name="qkv_proj",
    )(h, pos, freq, w_in)


def _neg_abs16(x):
    bits = lax.bitcast_convert_type(x, jnp.uint16) | jnp.uint16(0x8000)
    return lax.bitcast_convert_type(bits, BF16)


def _sb_attn_kernel(q_ref, k_ref, v_ref, o_ref, acc_ref, carry_ref, *, tq, tk):
    i = pl.program_id(2)
    ratio = tq // tk
    row = lax.broadcasted_iota(jnp.int32, (tk, tk), 0)
    col = lax.broadcasted_iota(jnp.int32, (tk, tk), 1)
    tri = jnp.where(row > col, 1.0, 0.0).astype(BF16)

    def scores(c, r0, masked):
        k = k_ref[0, pl.ds(pl.multiple_of(c * tk, tk), tk), :]
        zb = _dot_nt(q_ref[0, r0:, :], k).astype(BF16)
        softplus = jnp.log(1.0 + jnp.exp2(_neg_abs16(zb))) * LOG2E
        log_beta = jnp.minimum(zb, 0.0) - softplus
        log_om = log_beta - zb
        causal = None
        if masked:
            rows = tq - r0
            causal = (lax.broadcasted_iota(jnp.int32, (rows, tk), 1)
                      < lax.broadcasted_iota(jnp.int32, (rows, tk), 0))
            log_om = jnp.where(causal, log_om, jnp.zeros_like(log_om))
        within = _dot(log_om, tri)
        return log_beta, within, within[:, :1] + log_om[:, :1].astype(F32), causal

    def accumulate(c, r0, log_beta, within, total, causal):
        v = v_ref[0, pl.ds(pl.multiple_of(c * tk, tk), tk), :]
        w = jnp.exp2(log_beta.astype(F32) + within + jnp.tile(carry_ref[r0:, :], (1, tk // LANES)))
        if causal is not None:
            w = jnp.where(causal, w, 0.0)
        acc_ref[r0:, :] += _dot(w.astype(BF16), v)
        carry_ref[r0:, :] += total

    def run(blocks):
        staged = [scores(c, r0, masked) for c, r0, masked in blocks]
        for (c, r0, _), st in zip(blocks, staged):
            accumulate(c, r0, *st)

    acc_ref[...] = jnp.zeros_like(acc_ref)
    carry_ref[...] = jnp.zeros_like(carry_ref)
    run([(i * ratio + j, j * tk, True) for j in reversed(range(ratio))])

    @pl.loop(0, i)
    def _(t):
        run([((i - 1 - t) * ratio + j, 0, False) for j in reversed(range(ratio))])

    o_ref[0] = acc_ref[...].astype(o_ref.dtype)


def _sb_attention(qkv, *, n_heads, tq, tk):
    b, s, _ = qkv.shape
    return pl.pallas_call(
        functools.partial(_sb_attn_kernel, tq=tq, tk=tk),
        grid=(b, n_heads, s // tq),
        in_specs=[
            pl.BlockSpec((1, tq, HEAD_DIM), lambda bi, h, i: (bi, i, h)),
            pl.BlockSpec((1, s, HEAD_DIM), lambda bi, h, i: (bi, 0, n_heads + h)),
            pl.BlockSpec((1, s, HEAD_DIM), lambda bi, h, i: (bi, 0, 2 * n_heads + h)),
        ],
        out_specs=pl.BlockSpec((1, tq, HEAD_DIM), lambda bi, h, i: (bi, i, h)),
        out_shape=jax.ShapeDtypeStruct((b, s, n_heads * HEAD_DIM), BF16),
        scratch_shapes=[pltpu.VMEM((tq, HEAD_DIM), F32), pltpu.VMEM((tq, LANES), F32)],
        compiler_params=_params("parallel", "parallel", "arbitrary"),
        name="sb_attn",
    )(qkv, qkv, qkv)


def _diff_attn_kernel(q_ref, k_ref, v_ref, lq1_ref, lk1_ref, lq2_ref, lk2_ref, g_ref, o_ref,
                      qs_ref, m_ref, acc_ref, *, tq, tk, lambda_init):
    i = pl.program_id(2)
    ratio = tq // tk
    q = q_ref[0]
    lane = lax.broadcasted_iota(jnp.int32, (tq, HEAD_DIM), 1)
    zero = jnp.zeros_like(q)
    qs_ref[0] = jnp.where(lane < DIFF_QK_DIM, q, zero)
    qs_ref[1] = jnp.where(lane >= DIFF_QK_DIM, q, zero)
    m_ref[...] = jnp.full_like(m_ref, -jnp.inf)
    acc_ref[...] = jnp.zeros_like(acc_ref)

    def block(c, r0, masked):
        start = pl.multiple_of(c * tk, tk)
        k = k_ref[0, pl.ds(start, tk), :]
        v_ones = jnp.concatenate([v_ref[0, pl.ds(start, tk), :], jnp.ones((tk, LANES), BF16)], axis=1)
        for comp in range(2):
            sc = _dot_nt(qs_ref[comp, r0:, :], k)
            if masked:
                rows = tq - r0
                causal = (lax.broadcasted_iota(jnp.int32, (rows, tk), 1)
                          <= lax.broadcasted_iota(jnp.int32, (rows, tk), 0))
                sc = jnp.where(causal, sc, -jnp.inf)
            m_old = m_ref[comp, r0:, :]
            m_new = jnp.maximum(m_old, jnp.max(sc, axis=-1, keepdims=True))
            scale = jnp.exp2(m_old - m_new)
            pr = jnp.exp2(sc - jnp.tile(m_new, (1, tk // LANES)))
            m_ref[comp, r0:, :] = m_new
            acc_ref[comp, r0:, :] = (jnp.tile(scale, (1, 2)) * acc_ref[comp, r0:, :]
                                     + _dot(pr.astype(BF16), v_ones))

    for j in reversed(range(ratio)):
        block(i * ratio + j, j * tk, True)

    @pl.loop(0, i)
    def _(t):
        for j in reversed(range(ratio)):
            block((i - 1 - t) * ratio + j, 0, False)

    lam = (jnp.exp(jnp.sum(lq1_ref[...] * lk1_ref[...], axis=-1, keepdims=True))
           - jnp.exp(jnp.sum(lq2_ref[...] * lk2_ref[...], axis=-1, keepdims=True))
           + lambda_init)
    o = (acc_ref[0, :, :HEAD_DIM] / acc_ref[0, :, HEAD_DIM:]
         - lam * (acc_ref[1, :, :HEAD_DIM] / acc_ref[1, :, HEAD_DIM:]))
    o = o * lax.rsqrt(jnp.mean(o * o, axis=-1, keepdims=True) + RMS_EPS)
    o_ref[0] = (o * g_ref[...] * (1.0 - lambda_init)).astype(o_ref.dtype)


def _diff_attention(qkv, lq1, lk1, lq2, lk2, subln_g, *, n_heads, col0, lambda_init, tq, tk):
    b, s, _ = qkv.shape
    small = lambda width: pl.BlockSpec((1, width), lambda bi, h, i: (0, 0))
    return pl.pallas_call(
        functools.partial(_diff_attn_kernel, tq=tq, tk=tk, lambda_init=lambda_init),
        grid=(b, n_heads, s // tq),
        in_specs=[
            pl.BlockSpec((1, tq, HEAD_DIM), lambda bi, h, i: (bi, i, col0 + h)),
            pl.BlockSpec((1, s, HEAD_DIM), lambda bi, h, i: (bi, 0, col0 + n_heads + h)),
            pl.BlockSpec((1, s, HEAD_DIM), lambda bi, h, i: (bi, 0, col0 + 2 * n_heads + h)),
            small(DIFF_QK_DIM), small(DIFF_QK_DIM), small(DIFF_QK_DIM), small(DIFF_QK_DIM),
            small(HEAD_DIM),
        ],
        out_specs=pl.BlockSpec((1, tq, HEAD_DIM), lambda bi, h, i: (bi, i, h)),
        out_shape=jax.ShapeDtypeStruct((b, s, n_heads * HEAD_DIM), BF16),
        scratch_shapes=[pltpu.VMEM((2, tq, HEAD_DIM), BF16), pltpu.VMEM((2, tq, LANES), F32),
                        pltpu.VMEM((2, tq, HEAD_DIM + LANES), F32)],
        compiler_params=_params("parallel", "parallel", "arbitrary"),
        name="diff_attn",
    )(qkv, qkv, qkv, lq1, lk1, lq2, lk2, subln_g)


def _outproj_ln_kernel(h_ref, osb_ref, odf_ref, wa_ref, wb_ref, g_ref, b_ref, o_ref, *, alpha):
    tm = o_ref.shape[0]
    for r in range(0, tm, tm // ROW_GROUPS):
        rows = slice(r, r + tm // ROW_GROUPS)
        mix = _dot(osb_ref[rows, :], wa_ref[...]) + _dot(odf_ref[rows, :], wb_ref[...])
        o_ref[rows, :] = _layernorm_rows(alpha * h_ref[rows, :] + mix, g_ref[...], b_ref[...])


def _outproj_ln(h, o_sb, o_df, w_a, w_b, g, b, *, alpha, tm):
    n, d = h.shape
    wa_rows, wb_rows = w_a.shape[0], w_b.shape[0]
    return pl.pallas_call(
        functools.partial(_outproj_ln_kernel, alpha=alpha),
        grid=(n // tm,),
        in_specs=[
            pl.BlockSpec((tm, d), lambda i: (i, 0)),
            pl.BlockSpec((tm, wa_rows), lambda i: (i, 0)),
            pl.BlockSpec((tm, wb_rows), lambda i: (i, 0)),
            pl.BlockSpec((wa_rows, d), lambda i: (0, 0)),
            pl.BlockSpec((wb_rows, d), lambda i: (0, 0)),
            pl.BlockSpec((1, d), lambda i: (0, 0)),
            pl.BlockSpec((1, d), lambda i: (0, 0)),
        ],
        out_specs=pl.BlockSpec((tm, d), lambda i: (i, 0)),
        out_shape=jax.ShapeDtypeStruct((n, d), F32),
        compiler_params=_params("parallel"),
        name="outproj_ln",
    )(h, o_sb, o_df, w_a, w_b, g, b)


def _ple_ln_kernel(h_ref, hb_ref, p_ref, wg_ref, bg_ref, wp_ref, g_ref, b_ref, o_ref, *, alpha):
    tm = o_ref.shape[0]
    for r in range(0, tm, tm // ROW_GROUPS):
        rows = slice(r, r + tm // ROW_GROUPS)
        gate = jax.nn.sigmoid(_dot(hb_ref[rows, :], wg_ref[...]) + bg_ref[...])
        emb = _dot(p_ref[rows, :].astype(BF16), wp_ref[...])
        o_ref[rows, :] = _layernorm_rows(alpha * h_ref[rows, :] + gate * emb, g_ref[...], b_ref[...])


def _ple_ln(h, hb, p, wg, bg, wp, g, b, *, alpha, tm):
    n, d = h.shape
    dp = p.shape[1]
    return pl.pallas_call(
        functools.partial(_ple_ln_kernel, alpha=alpha),
        grid=(n // tm,),
        in_specs=[
            pl.BlockSpec((tm, d), lambda i: (i, 0)),
            pl.BlockSpec((tm, d), lambda i: (i, 0)),
            pl.BlockSpec((tm, dp), lambda i: (i, 0)),
            pl.BlockSpec((d, d), lambda i: (0, 0)),
            pl.BlockSpec((1, d), lambda i: (0, 0)),
            pl.BlockSpec((dp, d), lambda i: (0, 0)),
            pl.BlockSpec((1, d), lambda i: (0, 0)),
            pl.BlockSpec((1, d), lambda i: (0, 0)),
        ],
        out_specs=pl.BlockSpec((tm, d), lambda i: (i, 0)),
        out_shape=jax.ShapeDtypeStruct((n, d), F32),
        compiler_params=_params("parallel"),
        name="ple_ln",
    )(h, hb, p, wg, bg, wp, g, b)


def _tile(n, target):
    t = min(n, target)
    assert n % t == 0, (n, t)
    return t


def kernel(x, p, positions, ln_g, ln_b, ffn1_w_gate, ffn1_w_up, ffn1_w_down, w_in, w_out,
           lambda_q1, lambda_k1, lambda_q2, lambda_k2, diff_subln_g,
           ffn2_w_gate, ffn2_w_up, ffn2_w_down, ple_w_gate, ple_b_gate, ple_w_proj):
    batch, seq, d = x.shape
    depth = ln_g.shape[0]
    n = batch * seq
    n_sb = d // (2 * HEAD_DIM)
    n_diff = d // (2 * HEAD_DIM)
    sb_width = n_sb * HEAD_DIM
    alpha = (2.0 * depth) ** 0.25

    tm_ffn = _tile(n, 512)
    tf = _tile(ffn1_w_gate.shape[2], 256)
    tm_proj = _tile(n, 1024)
    tm_row = _tile(n, 512)
    tq = _tile(seq, 2048)
    tk = _tile(tq, 256)

    inv_freq = ROPE_THETA ** (-jnp.arange(0, ROT_DIM, 2, dtype=F32) / ROT_DIM)
    lane = jnp.arange(LANES) % DIFF_QK_DIM
    freq = jnp.where(lane < ROT_DIM, inv_freq[lane % (ROT_DIM // 2)], 0.0).reshape(1, LANES)
    pos = positions.reshape(n, 1)

    row = lambda v: v.reshape(1, -1)
    h = x.reshape(n, d)
    for i in range(depth):
        lambda_init = 0.8 - 0.6 * math.exp(-0.3 * i)
        h, hb = _ffn_ln(h, ffn1_w_gate[i].astype(BF16), ffn1_w_up[i].astype(BF16),
                        ffn1_w_down[i].astype(BF16), row(ln_g[i, 0]), row(ln_b[i, 0]),
                        alpha=alpha, tm=tm_ffn, tf=tf)
        qkv = _qkv_proj(hb, pos, freq, w_in[i].astype(BF16), tm=tm_proj)
        qkv = qkv.reshape(batch, seq, -1)
        o_sb = _sb_attention(qkv, n_heads=n_sb, tq=tq, tk=tk)
        o_df = _diff_attention(qkv, row(lambda_q1[i]), row(lambda_k1[i]), row(lambda_q2[i]),
                               row(lambda_k2[i]), row(diff_subln_g[i]), n_heads=n_diff,
                               col0=3 * n_sb, lambda_init=lambda_init, tq=tq, tk=tk)
        w_o = w_out[i].astype(BF16)
        h = _outproj_ln(h, o_sb.reshape(n, -1), o_df.reshape(n, -1), w_o[:sb_width], w_o[sb_width:],
                        row(ln_g[i, 1]), row(ln_b[i, 1]), alpha=alpha, tm=tm_row)
        h, hb = _ffn_ln(h, ffn2_w_gate[i].astype(BF16), ffn2_w_up[i].astype(BF16),
                        ffn2_w_down[i].astype(BF16), row(ln_g[i, 2]), row(ln_b[i, 2]),
                        alpha=alpha, tm=tm_ffn, tf=tf)
        h = _ple_ln(h, hb, p[i].reshape(n, -1), ple_w_gate[i].astype(BF16), row(ple_b_gate[i]),
                    ple_w_proj[i].astype(BF16), row(ln_g[i, 3]), row(ln_b[i, 3]),
                    alpha=alpha, tm=tm_row)
    return h.reshape(batch, seq, d)
```

```python
import functools
import math

import jax
import jax.numpy as jnp
from jax import lax
from jax.experimental import pallas as pl
from jax.experimental.pallas import tpu as pltpu

HEAD_DIM = 128
DIFF_QK_DIM = HEAD_DIM // 2
ROT_DIM = DIFF_QK_DIM // 4
ROPE_THETA = 500000.0
LN_EPS = 1e-5
RMS_EPS = 1e-5

LANES = 128
VMEM_LIMIT_BYTES = 60 * 1024 * 1024

BF16 = jnp.bfloat16
F32 = jnp.float32

LOG2E = 1.4426950408889634
ROW_GROUPS = 2

_NT = (((1,), (1,)), ((), ()))


def _dot(a, b):
    return jnp.dot(a, b, preferred_element_type=F32)


def _dot_nt(a, b):
    return lax.dot_general(a, b, _NT, preferred_element_type=F32)


def _layernorm_rows(y, g, b):
    mu = jnp.mean(y, axis=-1, keepdims=True)
    yc = y - mu
    var = jnp.mean(yc * yc, axis=-1, keepdims=True)
    return yc * lax.rsqrt(var + LN_EPS) * g + b


def _params(*semantics):
    return pltpu.CompilerParams(dimension_semantics=semantics,
                                vmem_limit_bytes=VMEM_LIMIT_BYTES)


def _ffn_ln_kernel(x_ref, wg_ref, wu_ref, wd_ref, g_ref, b_ref, o_ref, ob_ref, *, alpha, tf):
    xb = x_ref[...].astype(BF16)
    for c in range(wg_ref.shape[1] // tf):
        cols = slice(c * tf, (c + 1) * tf)
        gate = _dot(xb, wg_ref[:, cols])
        up = _dot(xb, wu_ref[:, cols])
        act = (gate * jax.nn.sigmoid(gate) * up).astype(BF16)
        part = _dot(act, wd_ref[cols, :])
        if c == 0:
            o_ref[...] = part
        else:
            o_ref[...] += part
    out = _layernorm_rows(alpha * x_ref[...] + 0.5 * o_ref[...], g_ref[...], b_ref[...])
    o_ref[...] = out
    ob_ref[...] = out.astype(BF16)


def _ffn_ln(h, wg, wu, wd, g, b, *, alpha, tm, tf):
    n, d = h.shape
    f = wg.shape[1]
    resident = lambda shape: pl.BlockSpec(shape, lambda i: (0, 0), pipeline_mode=pl.Buffered(1))
    return pl.pallas_call(
        functools.partial(_ffn_ln_kernel, alpha=alpha, tf=tf),
        grid=(n // tm,),
        in_specs=[
            pl.BlockSpec((tm, d), lambda i: (i, 0)),
            resident((d, f)),
            resident((d, f)),
            resident((f, d)),
            resident((1, d)),
            resident((1, d)),
        ],
        out_specs=[pl.BlockSpec((tm, d), lambda i: (i, 0)),
                   pl.BlockSpec((tm, d), lambda i: (i, 0))],
        out_shape=[jax.ShapeDtypeStruct((n, d), F32), jax.ShapeDtypeStruct((n, d), BF16)],
        compiler_params=_params("parallel"),
        name="ffn_ln",
    )(h, wg, wu, wd, g, b)


def _qkv_kernel(h_ref, pos_ref, freq_ref, w_ref, o_ref, *, sb_scale, diff_scale):
    tn = w_ref.shape[1] // 6
    half = ROT_DIM // 2
    hb = h_ref[...]
    lane = lax.broadcasted_iota(jnp.int32, (1, LANES), 1) % DIFF_QK_DIM
    ang = pos_ref[...].astype(F32) * freq_ref[...]
    cos = jnp.cos(ang)
    sin = jnp.sin(ang)
    c = jnp.where(lane < ROT_DIM, cos, 1.0)
    sa = jnp.where(lane < half, -sin, 0.0)
    sb = jnp.where((lane >= half) & (lane < ROT_DIM), sin, 0.0)
    for j in range(6):
        t = _dot(hb, w_ref[:, j * tn:(j + 1) * tn])
        if j in (3, 4):
            scale = diff_scale if j == 3 else 1.0
            for hh in range(tn // LANES):
                th = t[:, hh * LANES:(hh + 1) * LANES]
                t_up = pltpu.roll(th, LANES - half, axis=1)
                t_dn = pltpu.roll(th, half, axis=1)
                lo = j * tn + hh * LANES
                o_ref[:, lo:lo + LANES] = ((th * c + t_up * sa + t_dn * sb) * scale).astype(o_ref.dtype)
        else:
            scale = sb_scale if j == 0 else 1.0
            o_ref[:, j * tn:(j + 1) * tn] = (t * scale).astype(o_ref.dtype)


def _qkv_proj(h, pos, freq, w_in, *, tm):
    n, d = h.shape
    width = w_in.shape[1]
    return pl.pallas_call(
        functools.partial(_qkv_kernel, sb_scale=HEAD_DIM ** -0.5 * LOG2E,
                          diff_scale=DIFF_QK_DIM ** -0.5 * LOG2E),
        grid=(n // tm,),
        in_specs=[
            pl.BlockSpec((tm, d), lambda i: (i, 0)),
            pl.BlockSpec((tm, 1), lambda i: (i, 0)),
            pl.BlockSpec((1, LANES), lambda i: (0, 0)),
            pl.BlockSpec((d, width), lambda i: (0, 0), pipeline_mode=pl.Buffered(1)),
        ],
        out_specs=pl.BlockSpec((tm, width), lambda i: (i, 0)),
        out_shape=jax.ShapeDtypeStruct((n, width), BF16),
        compiler_params=_params("parallel"),
        name="qkv_proj",
    )(h, pos, freq, w_in)


def _neg_abs16(x):
    bits = lax.bitcast_convert_type(x, jnp.uint16) | jnp.uint16(0x8000)
    return lax.bitcast_convert_type(bits, BF16)


def _sb_attn_kernel(q_ref, k_ref, v_ref, o_ref, acc_ref, carry_ref, *, tq, tk):
    i = pl.program_id(2)
    ratio = tq // tk
    row = lax.broadcasted_iota(jnp.int32, (tk, tk), 0)
    col = lax.broadcasted_iota(jnp.int32, (tk, tk), 1)
    tri = jnp.where(row > col, 1.0, 0.0).astype(BF16)

    def scores(c, r0, masked):
        k = k_ref[0, pl.ds(pl.multiple_of(c * tk, tk), tk), :]
        zb = _dot_nt(q_ref[0, r0:, :], k).astype(BF16)
        softplus = jnp.log(1.0 + jnp.exp2(_neg_abs16(zb))) * LOG2E
        log_beta = jnp.minimum(zb, 0.0) - softplus
        log_om = log_beta - zb
        causal = None
        if masked:
            rows = tq - r0
            causal = (lax.broadcasted_iota(jnp.int32, (rows, tk), 1)
                      < lax.broadcasted_iota(jnp.int32, (rows, tk), 0))
            log_om = jnp.where(causal, log_om, jnp.zeros_like(log_om))
        within = _dot(log_om, tri)
        return log_beta, within, within[:, :1] + log_om[:, :1].astype(F32), causal

    def accumulate(c, r0, log_beta, within, total, causal):
        v = v_ref[0, pl.ds(pl.multiple_of(c * tk, tk), tk), :]
        w = jnp.exp2(log_beta.astype(F32) + within + jnp.tile(carry_ref[r0:, :], (1, tk // LANES)))
        if causal is not None:
            w = jnp.where(causal, w, 0.0)
        acc_ref[r0:, :] += _dot(w.astype(BF16), v)
        carry_ref[r0:, :] += total

    def run(blocks):
        staged = [scores(c, r0, masked) for c, r0, masked in blocks]
        for (c, r0, _), st in zip(blocks, staged):
            accumulate(c, r0, *st)

    acc_ref[...] = jnp.zeros_like(acc_ref)
    carry_ref[...] = jnp.zeros_like(carry_ref)
    run([(i * ratio + j, j * tk, True) for j in reversed(range(ratio))])

    @pl.loop(0, i)
    def _(t):
        run([((i - 1 - t) * ratio + j, 0, False) for j in reversed(range(ratio))])

    o_ref[0] = acc_ref[...].astype(o_ref.dtype)


def _sb_attention(qkv, *, n_heads, tq, tk):
    b, s, _ = qkv.shape
    return pl.pallas_call(
        functools.partial(_sb_attn_kernel, tq=tq, tk=tk),
        grid=(b, n_heads, s // tq),
        in_specs=[
            pl.BlockSpec((1, tq, HEAD_DIM), lambda bi, h, i: (bi, i, h)),
            pl.BlockSpec((1, s, HEAD_DIM), lambda bi, h, i: (bi, 0, n_heads + h)),
            pl.BlockSpec((1, s, HEAD_DIM), lambda bi, h, i: (bi, 0, 2 * n_heads + h)),
        ],
        out_specs=pl.BlockSpec((1, tq, HEAD_DIM), lambda bi, h, i: (bi, i, h)),
        out_shape=jax.ShapeDtypeStruct((b, s, n_heads * HEAD_DIM), BF16),
        scratch_shapes=[pltpu.VMEM((tq, HEAD_DIM), F32), pltpu.VMEM((tq, LANES), F32)],
        compiler_params=_params("parallel", "parallel", "arbitrary"),
        name="sb_attn",
    )(qkv, qkv, qkv)


def _diff_attn_kernel(q_ref, k_ref, v_ref, lq1_ref, lk1_ref, lq2_ref, lk2_ref, g_ref, o_ref,
                      qs_ref, m_ref, acc_ref, *, tq, tk, lambda_init):
    i = pl.program_id(2)
    ratio = tq // tk
    q = q_ref[0]
    lane = lax.broadcasted_iota(jnp.int32, (tq, HEAD_DIM), 1)
    zero = jnp.zeros_like(q)
    qs_ref[0] = jnp.where(lane < DIFF_QK_DIM, q, zero)
    qs_ref[1] = jnp.where(lane >= DIFF_QK_DIM, q, zero)
    m_ref[...] = jnp.full_like(m_ref, -jnp.inf)
    acc_ref[...] = jnp.zeros_like(acc_ref)

    def block(c, r0, masked):
        start = pl.multiple_of(c * tk, tk)
        k = k_ref[0, pl.ds(start, tk), :]
        v_ones = jnp.concatenate([v_ref[0, pl.ds(start, tk), :], jnp.ones((tk, LANES), BF16)], axis=1)
        for comp in range(2):
            sc = _dot_nt(qs_ref[comp, r0:, :], k)
            if masked:
                rows = tq - r0
                causal = (lax.broadcasted_iota(jnp.int32, (rows, tk), 1)
                          <= lax.broadcasted_iota(jnp.int32, (rows, tk), 0))
                sc = jnp.where(causal, sc, -jnp.inf)
            m_old = m_ref[comp, r0:, :]
            m_new = jnp.maximum(m_old, jnp.max(sc, axis=-1, keepdims=True))
            scale = jnp.exp2(m_old - m_new)
            pr = jnp.exp2(sc - jnp.tile(m_new, (1, tk // LANES)))
            m_ref[comp, r0:, :] = m_new
            acc_ref[comp, r0:, :] = (jnp.tile(scale, (1, 2)) * acc_ref[comp, r0:, :]
                                     + _dot(pr.astype(BF16), v_ones))

    for j in reversed(range(ratio)):
        block(i * ratio + j, j * tk, True)

    @pl.loop(0, i)
    def _(t):
        for j in reversed(range(ratio)):
            block((i - 1 - t) * ratio + j, 0, False)

    lam = (jnp.exp(jnp.sum(lq1_ref[...] * lk1_ref[...], axis=-1, keepdims=True))
           - jnp.exp(jnp.sum(lq2_ref[...] * lk2_ref[...], axis=-1, keepdims=True))
           + lambda_init)
    o = (acc_ref[0, :, :HEAD_DIM] / acc_ref[0, :, HEAD_DIM:]
         - lam * (acc_ref[1, :, :HEAD_DIM] / acc_ref[1, :, HEAD_DIM:]))
    o = o * lax.rsqrt(jnp.mean(o * o, axis=-1, keepdims=True) + RMS_EPS)
    o_ref[0] = (o * g_ref[...] * (1.0 - lambda_init)).astype(o_ref.dtype)


def _diff_attention(qkv, lq1, lk1, lq2, lk2, subln_g, *, n_heads, col0, lambda_init, tq, tk):
    b, s, _ = qkv.shape
    small = lambda width: pl.BlockSpec((1, width), lambda bi, h, i: (0, 0))
    return pl.pallas_call(
        functools.partial(_diff_attn_kernel, tq=tq, tk=tk, lambda_init=lambda_init),
        grid=(b, n_heads, s // tq),
        in_specs=[
            pl.BlockSpec((1, tq, HEAD_DIM), lambda bi, h, i: (bi, i, col0 + h)),
            pl.BlockSpec((1, s, HEAD_DIM), lambda bi, h, i: (bi, 0, col0 + n_heads + h)),
            pl.BlockSpec((1, s, HEAD_DIM), lambda bi, h, i: (bi, 0, col0 + 2 * n_heads + h)),
            small(DIFF_QK_DIM), small(DIFF_QK_DIM), small(DIFF_QK_DIM), small(DIFF_QK_DIM),
            small(HEAD_DIM),
        ],
        out_specs=pl.BlockSpec((1, tq, HEAD_DIM), lambda bi, h, i: (bi, i, h)),
        out_shape=jax.ShapeDtypeStruct((b, s, n_heads * HEAD_DIM), BF16),
        scratch_shapes=[pltpu.VMEM((2, tq, HEAD_DIM), BF16), pltpu.VMEM((2, tq, LANES), F32),
                        pltpu.VMEM((2, tq, HEAD_DIM + LANES), F32)],
        compiler_params=_params("parallel", "parallel", "arbitrary"),
        name="diff_attn",
    )(qkv, qkv, qkv, lq1, lk1, lq2, lk2, subln_g)


def _outproj_ln_kernel(h_ref, osb_ref, odf_ref, wa_ref, wb_ref, g_ref, b_ref, o_ref, *, alpha):
    tm = o_ref.shape[0]
    for r in range(0, tm, tm // ROW_GROUPS):
        rows = slice(r, r + tm // ROW_GROUPS)
        mix = _dot(osb_ref[rows, :], wa_ref[...]) + _dot(odf_ref[rows, :], wb_ref[...])
        o_ref[rows, :] = _layernorm_rows(alpha * h_ref[rows, :] + mix, g_ref[...], b_ref[...])


def _outproj_ln(h, o_sb, o_df, w_a, w_b, g, b, *, alpha, tm):
    n, d = h.shape
    wa_rows, wb_rows = w_a.shape[0], w_b.shape[0]
    return pl.pallas_call(
        functools.partial(_outproj_ln_kernel, alpha=alpha),
        grid=(n // tm,),
        in_specs=[
            pl.BlockSpec((tm, d), lambda i: (i, 0)),
            pl.BlockSpec((tm, wa_rows), lambda i: (i, 0)),
            pl.BlockSpec((tm, wb_rows), lambda i: (i, 0)),
            pl.BlockSpec((wa_rows, d), lambda i: (0, 0)),
            pl.BlockSpec((wb_rows, d), lambda i: (0, 0)),
            pl.BlockSpec((1, d), lambda i: (0, 0)),
            pl.BlockSpec((1, d), lambda i: (0, 0)),
        ],
        out_specs=pl.BlockSpec((tm, d), lambda i: (i, 0)),
        out_shape=jax.ShapeDtypeStruct((n, d), F32),
        compiler_params=_params("parallel"),
        name="outproj_ln",
    )(h, o_sb, o_df, w_a, w_b, g, b)


def _ple_ln_kernel(h_ref, hb_ref, p_ref, wg_ref, bg_ref, wp_ref, g_ref, b_ref, o_ref, *, alpha):
    tm = o_ref.shape[0]
    for r in range(0, tm, tm // ROW_GROUPS):
        rows = slice(r, r + tm // ROW_GROUPS)
        gate = jax.nn.sigmoid(_dot(hb_ref[rows, :], wg_ref[...]) + bg_ref[...])
        emb = _dot(p_ref[rows, :].astype(BF16), wp_ref[...])
        o_ref[rows, :] = _layernorm_rows(alpha * h_ref[rows, :] + gate * emb, g_ref[...], b_ref[...])


def _ple_ln(h, hb, p, wg, bg, wp, g, b, *, alpha, tm):
    n, d = h.shape
    dp = p.shape[1]
    return pl.pallas_call(
        functools.partial(_ple_ln_kernel, alpha=alpha),
        grid=(n // tm,),
        in_specs=[
            pl.BlockSpec((tm, d), lambda i: (i, 0)),
            pl.BlockSpec((tm, d), lambda i: (i, 0)),
            pl.BlockSpec((tm, dp), lambda i: (i, 0)),
            pl.BlockSpec((d, d), lambda i: (0, 0)),
            pl.BlockSpec((1, d), lambda i: (0, 0)),
            pl.BlockSpec((dp, d), lambda i: (0, 0)),
            pl.BlockSpec((1, d), lambda i: (0, 0)),
            pl.BlockSpec((1, d), lambda i: (0, 0)),
        ],
        out_specs=pl.BlockSpec((tm, d), lambda i: (i, 0)),
        out_shape=jax.ShapeDtypeStruct((n, d), F32),
        compiler_params=_params("parallel"),
        name="ple_ln",
    )(h, hb, p, wg, bg, wp, g, b)


def _tile(n, target):
    t = min(n, target)
    assert n % t == 0, (n, t)
    return t


def kernel(x, p, positions, ln_g, ln_b, ffn1_w_gate, ffn1_w_up, ffn1_w_down, w_in, w_out,
           lambda_q1, lambda_k1, lambda_q2, lambda_k2, diff_subln_g,
           ffn2_w_gate, ffn2_w_up, ffn2_w_down, ple_w_gate, ple_b_gate, ple_w_proj):
    batch, seq, d = x.shape
    depth = ln_g.shape[0]
    n = batch * seq
    n_sb = d // (2 * HEAD_DIM)
    n_diff = d // (2 * HEAD_DIM)
    sb_width = n_sb * HEAD_DIM
    alpha = (2.0 * depth) ** 0.25

    tm_ffn = _tile(n, 512)
    tf = _tile(ffn1_w_gate.shape[2], 256)
    tm_proj = _tile(n, 512)
    tm_row = _tile(n, 512)
    tq = _tile(seq, 2048)
    tk = _tile(tq, 256)

    inv_freq = ROPE_THETA ** (-jnp.arange(0, ROT_DIM, 2, dtype=F32) / ROT_DIM)
    lane = jnp.arange(LANES) % DIFF_QK_DIM
    freq = jnp.where(lane < ROT_DIM, inv_freq[lane % (ROT_DIM // 2)], 0.0).reshape(1, LANES)
    pos = positions.reshape(n, 1)

    row = lambda v: v.reshape(1, -1)
    h = x.reshape(n, d)
    for i in range(depth):
        lambda_init = 0.8 - 0.6 * math.exp(-0.3 * i)
        h, hb = _ffn_ln(h, ffn1_w_gate[i].astype(BF16), ffn1_w_up[i].astype(BF16),
                        ffn1_w_down[i].astype(BF16), row(ln_g[i, 0]), row(ln_b[i, 0]),
                        alpha=alpha, tm=tm_ffn, tf=tf)
        qkv = _qkv_proj(hb, pos, freq, w_in[i].astype(BF16), tm=tm_proj)
        qkv = qkv.reshape(batch, seq, -1)
        o_sb = _sb_attention(qkv, n_heads=n_sb, tq=tq, tk=tk)
        o_df = _diff_attention(qkv, row(lambda_q1[i]), row(lambda_k1[i]), row(lambda_q2[i]),
                               row(lambda_k2[i]), row(diff_subln_g[i]), n_heads=n_diff,
                               col0=3 * n_sb, lambda_init=lambda_init, tq=tq, tk=tk)
        w_o = w_out[i].astype(BF16)
        h = _outproj_ln(h, o_sb.reshape(n, -1), o_df.reshape(n, -1), w_o[:sb_width], w_o[sb_width:],
                        row(ln_g[i, 1]), row(ln_b[i, 1]), alpha=alpha, tm=tm_row)
        h, hb = _ffn_ln(h, ffn2_w_gate[i].astype(BF16), ffn2_w_up[i].astype(BF16),
                        ffn2_w_down[i].astype(BF16), row(ln_g[i, 2]), row(ln_b[i, 2]),
                        alpha=alpha, tm=tm_ffn, tf=tf)
        h = _ple_ln(h, hb, p[i].reshape(n, -1), ple_w_gate[i].astype(BF16), row(ple_b_gate[i]),
                    ple_w_proj[i].astype(BF16), row(ln_g[i, 3]), row(ln_b[i, 3]),
                    alpha=alpha, tm=tm_row)
    return h.reshape(batch, seq, d)
```
